```python
import math
import jax, jax.numpy as jnp
from jax import lax
import numpy as np

D_MODEL = 1024
BATCH = 4
SEQ = 4096
DEPTH = 4

N_MIXERS = 4
BLOCK = 128
CONV_K = 4
ROPE_THETA = 10000.0
EPS = 1e-6
LN_EPS = 1e-5
DEEPNORM_ALPHA = (2 * DEPTH) ** 0.25
DEEPNORM_BETA = (8 * DEPTH) ** -0.25

GDN_HEADS = 8
GDN_DK = 128
GDN_DV = 128
GDN_CHUNK = 64
GDN_QK = GDN_HEADS * GDN_DK
GDN_V = GDN_HEADS * GDN_DV
GDN_CONV = 2 * GDN_QK + GDN_V
GDN_IN = GDN_CONV + GDN_V + 2 * GDN_HEADS

SWA_HEADS = 16
SWA_KV_HEADS = 2
SWA_HD = 64
SWA_WINDOW = 128
SWA_Q = SWA_HEADS * SWA_HD
SWA_KV = SWA_KV_HEADS * SWA_HD
SWA_IN = SWA_Q + 2 * SWA_KV + SWA_Q

FOX_HEADS = 16
FOX_HD = 64
FOX_W = FOX_HEADS * FOX_HD
FOX_IN = 4 * FOX_W + FOX_HEADS

MLSTM_HEADS = 8
MLSTM_DQK = 64
MLSTM_DV = 128
MLSTM_CHUNK = 64
MLSTM_QKW = MLSTM_HEADS * MLSTM_DQK
MLSTM_V = MLSTM_HEADS * MLSTM_DV
MLSTM_IN = 2 * MLSTM_QKW + 3 * MLSTM_V + 2 * MLSTM_HEADS

kernel_name = "hybrid_interleaved_gdn_swa_fox_mlstm"


def _n_of(kind):
    return (DEPTH - kind + N_MIXERS - 1) // N_MIXERS


def _rmsnorm(x, g):
    xf = x.astype(jnp.float32)
    y = xf * lax.rsqrt(jnp.mean(xf * xf, axis=-1, keepdims=True) + EPS)
    return (y * g.astype(jnp.float32)).astype(x.dtype)


def _layernorm(x, g, b):
    xf = x.astype(jnp.float32)
    mu = jnp.mean(xf, axis=-1, keepdims=True)
    var = jnp.mean(jnp.square(xf - mu), axis=-1, keepdims=True)
    y = (xf - mu) * lax.rsqrt(var + LN_EPS) * g.astype(jnp.float32) + b.astype(jnp.float32)
    return y.astype(x.dtype)


def _l2norm(x):
    return x * lax.rsqrt(jnp.sum(x * x, axis=-1, keepdims=True) + EPS)


def _causal_conv_silu(x, w):
    ch = x.shape[-1]
    y = lax.conv_general_dilated(
        x, w[:, None, :].astype(x.dtype), window_strides=(1,),
        padding=[(CONV_K - 1, 0)], dimension_numbers=("NWC", "WIO", "NWC"),
        feature_group_count=ch)
    return jax.nn.silu(y)


def _rope(x, pos):
    half = x.shape[-1] // 2
    inv = ROPE_THETA ** (-jnp.arange(half, dtype=jnp.float32) / half)
    ang = pos.astype(jnp.float32)[:, None] * inv[None, :]
    cos = jnp.cos(ang)[None, :, None, :]
    sin = jnp.sin(ang)[None, :, None, :]
    xf = x.astype(jnp.float32)
    x1, x2 = xf[..., :half], xf[..., half:]
    return jnp.concatenate([x1 * cos - x2 * sin, x2 * cos + x1 * sin], axis=-1).astype(x.dtype)


def _chunk_heads(t, n, c, h, d):
    return t.astype(jnp.float32).reshape(t.shape[0], n, c, h, d).transpose(0, 3, 1, 2, 4)


def _chunk_gate(g, n, c):
    return g.reshape(g.shape[0], n, c, g.shape[-1]).transpose(0, 3, 1, 2)


def _unchunk(o):
    n, b, h, c, d = o.shape
    return o.transpose(1, 0, 3, 2, 4).reshape(b, n * c, h, d)


def _gated_deltanet(x, w_in, conv_w, a_log, dt_bias, norm_g, w_out):
    f32 = jnp.float32
    bsz, t_len, _ = x.shape
    H, DK, DV, C = GDN_HEADS, GDN_DK, GDN_DV, GDN_CHUNK
    n = t_len // C
    qkv, z, beta_logit, a_logit = jnp.split(
        x @ w_in, [GDN_CONV, GDN_CONV + GDN_V, GDN_CONV + GDN_V + H], axis=-1)
    qkv = _causal_conv_silu(qkv, conv_w)
    q, k, v = jnp.split(qkv, [GDN_QK, 2 * GDN_QK], axis=-1)
    q = _l2norm(_chunk_heads(q, n, C, H, DK)) * DK ** -0.5
    k = _l2norm(_chunk_heads(k, n, C, H, DK))
    v = _chunk_heads(v, n, C, H, DV)
    beta = _chunk_gate(jax.nn.sigmoid(beta_logit.astype(f32)), n, C)
    g = -jnp.exp(a_log.astype(f32)) * jax.nn.softplus(a_logit.astype(f32) + dt_bias.astype(f32))
    gc = jnp.cumsum(_chunk_gate(g, n, C), axis=-1)
    causal = jnp.tril(jnp.ones((C, C), bool))
    strict = jnp.tril(jnp.ones((C, C), bool), -1)
    decay = jnp.exp(jnp.where(causal, gc[..., :, None] - gc[..., None, :], -jnp.inf))
    kb = k * beta[..., None]
    lmat = jnp.where(strict, jnp.einsum("bhnid,bhnjd->bhnij", kb, k) * decay, 0.0)
    rhs = jnp.concatenate([v * beta[..., None], kb * jnp.exp(gc)[..., None]], axis=-1)
    uw = lax.linalg.triangular_solve(lmat, rhs, left_side=True, lower=True, unit_diagonal=True)
    u, w = uw[..., :DV], uw[..., DV:]
    qk = jnp.where(causal, jnp.einsum("bhnid,bhnjd->bhnij", q, k) * decay, 0.0)
    q_dec = q * jnp.exp(gc)[..., None]
    g_last = gc[..., -1]
    k_dec = k * jnp.exp(g_last[..., None] - gc)[..., None]

    def step(state, xs):
        qk_n, q_n, u_n, w_n, kd_n, gl_n = xs
        v_new = u_n - jnp.einsum("bhcd,bhde->bhce", w_n, state)
        o = jnp.einsum("bhcd,bhde->bhce", q_n, state) + jnp.einsum("bhij,bhje->bhie", qk_n, v_new)
        state = state * jnp.exp(gl_n)[..., None, None] + jnp.einsum("bhcd,bhce->bhde", kd_n, v_new)
        return state, o

    xs = tuple(jnp.moveaxis(a, 2, 0) for a in (qk, q_dec, u, w, k_dec, g_last))
    _, o = lax.scan(step, jnp.zeros((bsz, H, DK, DV), f32), xs)
    o = _rmsnorm(_unchunk(o), norm_g).reshape(bsz, t_len, GDN_V)
    o = o * jax.nn.silu(z.astype(f32))
    return o.astype(x.dtype) @ w_out


def _swa_sinks(x, w_in, sinks, w_out, pos):
    f32 = jnp.float32
    bsz, t_len, _ = x.shape
    H, KVH, HD, W = SWA_HEADS, SWA_KV_HEADS, SWA_HD, SWA_WINDOW
    G = H // KVH
    nb = t_len // W
    q, k, v, z = jnp.split(x @ w_in, [SWA_Q, SWA_Q + SWA_KV, SWA_Q + 2 * SWA_KV], axis=-1)
    q = _rope(q.reshape(bsz, t_len, H, HD), pos)
    k = _rope(k.reshape(bsz, t_len, KVH, HD), pos)
    v = v.reshape(bsz, t_len, KVH, HD)
    qb = q.reshape(bsz, nb, W, KVH, G, HD)

    def with_prev(a):
        ab = a.reshape(bsz, nb, W, KVH, HD)
        prev = jnp.pad(ab, ((0, 0), (1, 0), (0, 0), (0, 0), (0, 0)))[:, :-1]
        return jnp.concatenate([prev, ab], axis=2)

    kb, vb = with_prev(k), with_prev(v)
    s = jnp.einsum("bnqhgd,bnkhd->bnhgqk", qb, kb).astype(f32) * HD ** -0.5
    qi = jnp.arange(W)[:, None]
    kj = jnp.arange(2 * W)[None, :]
    rel = qi + W - kj
    blk = jnp.arange(nb)[:, None, None]
    valid = (rel >= 0) & (rel < W) & (blk * W + kj - W >= 0)
    s = jnp.where(valid[None, :, None, None], s, -jnp.inf)
    sink = jnp.broadcast_to(sinks.astype(f32).reshape(KVH, G)[None, None, :, :, None, None],
                            s.shape[:-1] + (1,))
    p = jax.nn.softmax(jnp.concatenate([s, sink], axis=-1), axis=-1)[..., :-1]
    o = jnp.einsum("bnhgqk,bnkhd->bnqhgd", p.astype(vb.dtype), vb).reshape(bsz, t_len, SWA_Q)
    return (o * jax.nn.silu(z)) @ w_out


def _forgetting_attention(x, w_in, f_bias, q_norm_g, k_norm_g, w_out):
    f32 = jnp.float32
    bsz, t_len, _ = x.shape
    H, HD = FOX_HEADS, FOX_HD
    q, k, v, z, f_logit = jnp.split(x @ w_in, [FOX_W, 2 * FOX_W, 3 * FOX_W, 4 * FOX_W], axis=-1)
    q = _rmsnorm(q.reshape(bsz, t_len, H, HD), q_norm_g)
    k = _rmsnorm(k.reshape(bsz, t_len, H, HD), k_norm_g)
    v = v.reshape(bsz, t_len, H, HD)
    log_f = jax.nn.log_sigmoid(f_logit.astype(f32) + f_bias.astype(f32))
    cum = jnp.cumsum(log_f, axis=1).transpose(0, 2, 1)
    scale = HD ** -0.5
    outs = []
    for nblk in range(t_len // BLOCK):
        lo, hi = nblk * BLOCK, (nblk + 1) * BLOCK
        s = jnp.einsum("bqhd,bkhd->bhqk", q[:, lo:hi], k[:, :hi]).astype(f32) * scale
        bias = cum[:, :, lo:hi, None] - cum[:, :, None, :hi]
        mask = jnp.arange(lo, hi)[:, None] >= jnp.arange(hi)[None, :]
        p = jax.nn.softmax(jnp.where(mask, s + bias, -jnp.inf), axis=-1)
        outs.append(jnp.einsum("bhqk,bkhd->bqhd", p.astype(v.dtype), v[:, :hi]))
    o = jnp.concatenate(outs, axis=1).reshape(bsz, t_len, FOX_W)
    return (o * jax.nn.silu(z)) @ w_out


def _mlstm(x, w_in, conv_w, gate_bias, norm_g, w_out):
    f32 = jnp.float32
    bsz, t_len, _ = x.shape
    H, DQK, DV, C = MLSTM_HEADS, MLSTM_DQK, MLSTM_DV, MLSTM_CHUNK
    n = t_len // C
    qk, v, o_logit, z, gates = jnp.split(
        x @ w_in, [2 * MLSTM_QKW, 2 * MLSTM_QKW + MLSTM_V, 2 * MLSTM_QKW + 2 * MLSTM_V,
                   2 * MLSTM_QKW + 3 * MLSTM_V], axis=-1)
    q, k = jnp.split(_causal_conv_silu(qk, conv_w), 2, axis=-1)
    q = _chunk_heads(q, n, C, H, DQK) * DQK ** -0.5
    k = _chunk_heads(k, n, C, H, DQK)
    v = _chunk_heads(v, n, C, H, DV)
    gates = gates.astype(f32) + gate_bias.astype(f32)
    ig = _chunk_gate(gates[..., :H], n, C)
    lf = jax.nn.log_sigmoid(_chunk_gate(gates[..., H:], n, C))
    b = jnp.cumsum(lf, axis=-1)
    causal = jnp.tril(jnp.ones((C, C), bool))
    d_intra = jnp.where(causal, b[..., :, None] - b[..., None, :] + ig[..., None, :], -jnp.inf)
    m_intra = jnp.max(d_intra, axis=-1)
    qk_s = jnp.einsum("bhnid,bhnjd->bhnij", q, k)
    a_end = b[..., -1:] - b + ig
    m_end = jnp.max(a_end, axis=-1)

    def step(carry, xs):
        c_st, n_st, m_st = carry
        q_n, k_n, v_n, d_n, s_n, mi_n, b_n, ae_n, me_n = xs
        inter = b_n + m_st[..., None]
        m_t = jnp.maximum(inter, mi_n)
        p = s_n * jnp.exp(d_n - m_t[..., None])
        w_inter = jnp.exp(inter - m_t)
        num = w_inter[..., None] * jnp.einsum("bhcd,bhde->bhce", q_n, c_st) + jnp.einsum("bhij,bhje->bhie", p, v_n)
        den = w_inter * jnp.einsum("bhcd,bhd->bhc", q_n, n_st) + jnp.sum(p, axis=-1)
        h = num / jnp.maximum(jnp.abs(den), jnp.exp(-m_t))[..., None]
        m_new = jnp.maximum(b_n[..., -1] + m_st, me_n)
        w_old = jnp.exp(b_n[..., -1] + m_st - m_new)
        w_tok = jnp.exp(ae_n - m_new[..., None])
        c_st = w_old[..., None, None] * c_st + jnp.einsum("bhcd,bhce->bhde", k_n * w_tok[..., None], v_n)
        n_st = w_old[..., None] * n_st + jnp.einsum("bhcd,bhc->bhd", k_n, w_tok)
        return (c_st, n_st, m_new), h

    xs = tuple(jnp.moveaxis(a, 2, 0) for a in (q, k, v, d_intra, qk_s, m_intra, b, a_end, m_end))
    init = (jnp.zeros((bsz, H, DQK, DV), f32), jnp.zeros((bsz, H, DQK), f32), jnp.zeros((bsz, H), f32))
    _, h = lax.scan(step, init, xs)
    h = _rmsnorm(_unchunk(h), norm_g).reshape(bsz, t_len, MLSTM_V)
    h = h * jax.nn.sigmoid(o_logit.astype(f32)) * jax.nn.silu(z.astype(f32))
    return h.astype(x.dtype) @ w_out


def setup_inputs(seed: int = 0) -> dict:
    key = jax.random.key(seed)
    ks = iter(jax.random.split(key, 32))

    def nrm(shape, scale):
        return scale * jax.random.normal(next(ks), shape, jnp.float32)

    na, nb, nc, nd = (_n_of(m) for m in range(N_MIXERS))
    x = nrm((BATCH, SEQ, D_MODEL), 1.0)
    a_w_in = nrm((na, D_MODEL, GDN_IN), D_MODEL ** -0.5)
    a_conv = nrm((na, CONV_K, GDN_CONV), CONV_K ** -0.5)
    a_a_log = jnp.log(jax.random.uniform(next(ks), (na, GDN_HEADS), jnp.float32, 1.0, 16.0))
    dt = jnp.exp(jax.random.uniform(next(ks), (na, GDN_HEADS), jnp.float32, math.log(1e-3), math.log(1e-1)))
    a_dt_bias = dt + jnp.log(-jnp.expm1(-dt))
    a_norm_g = 1.0 + nrm((na, GDN_DV), 0.02)
    a_w_out = nrm((na, GDN_V, D_MODEL), GDN_V ** -0.5 * DEEPNORM_BETA)
    b_w_in = nrm((nb, D_MODEL, SWA_IN), D_MODEL ** -0.5)
    b_sinks = nrm((nb, SWA_HEADS), 0.5)
    b_w_out = nrm((nb, SWA_Q, D_MODEL), SWA_Q ** -0.5 * DEEPNORM_BETA)
    c_w_in = nrm((nc, D_MODEL, FOX_IN), D_MODEL ** -0.5)
    c_f_bias = 2.0 + nrm((nc, FOX_HEADS), 0.5)
    c_q_norm = 1.0 + nrm((nc, FOX_HD), 0.02)
    c_k_norm = 1.0 + nrm((nc, FOX_HD), 0.02)
    c_w_out = nrm((nc, FOX_W, D_MODEL), FOX_W ** -0.5 * DEEPNORM_BETA)
    d_w_in = nrm((nd, D_MODEL, MLSTM_IN), D_MODEL ** -0.5)
    d_conv = nrm((nd, CONV_K, 2 * MLSTM_QKW), CONV_K ** -0.5)
    d_gate_bias = jnp.concatenate([nrm((nd, MLSTM_HEADS), 0.1), 3.0 + nrm((nd, MLSTM_HEADS), 0.5)], axis=-1)
    d_norm_g = 1.0 + nrm((nd, MLSTM_DV), 0.02)
    d_w_out = nrm((nd, MLSTM_V, D_MODEL), MLSTM_V ** -0.5 * DEEPNORM_BETA)
    ln_g = 1.0 + nrm((DEPTH, D_MODEL), 0.02)
    ln_b = nrm((DEPTH, D_MODEL), 0.02)
    return {"x": x,
            "a_w_in": a_w_in, "a_conv": a_conv, "a_a_log": a_a_log, "a_dt_bias": a_dt_bias,
            "a_norm_g": a_norm_g, "a_w_out": a_w_out,
            "b_w_in": b_w_in, "b_sinks": b_sinks, "b_w_out": b_w_out,
            "c_w_in": c_w_in, "c_f_bias": c_f_bias, "c_q_norm": c_q_norm, "c_k_norm": c_k_norm,
            "c_w_out": c_w_out,
            "d_w_in": d_w_in, "d_conv": d_conv, "d_gate_bias": d_gate_bias, "d_norm_g": d_norm_g,
            "d_w_out": d_w_out,
            "ln_g": ln_g, "ln_b": ln_b}


def reference(x, a_w_in, a_conv, a_a_log, a_dt_bias, a_norm_g, a_w_out,
              b_w_in, b_sinks, b_w_out,
              c_w_in, c_f_bias, c_q_norm, c_k_norm, c_w_out,
              d_w_in, d_conv, d_gate_bias, d_norm_g, d_w_out,
              ln_g, ln_b):
    pos = jnp.arange(x.shape[1], dtype=jnp.int32)
    for i in range(DEPTH):
        kind, j = i % N_MIXERS, i // N_MIXERS
        if kind == 0:
            y = _gated_deltanet(x, a_w_in[j], a_conv[j], a_a_log[j], a_dt_bias[j], a_norm_g[j], a_w_out[j])
        elif kind == 1:
            y = _swa_sinks(x, b_w_in[j], b_sinks[j], b_w_out[j], pos)
        elif kind == 2:
            y = _forgetting_attention(x, c_w_in[j], c_f_bias[j], c_q_norm[j], c_k_norm[j], c_w_out[j])
        else:
            y = _mlstm(x, d_w_in[j], d_conv[j], d_gate_bias[j], d_norm_g[j], d_w_out[j])
        x = _layernorm(DEEPNORM_ALPHA * x + y, ln_g[i], ln_b[i])
    return x
```

```python
import functools
import math

import jax
import jax.numpy as jnp
from jax import lax
from jax.experimental import pallas as pl
from jax.experimental.pallas import tpu as pltpu

F32 = jnp.float32
BF16 = jnp.bfloat16
LANES = 128
VMEM_LIMIT = 56 * 1024 * 1024

D_MODEL = 1024
DEPTH = 4
CONV_K = 4
ROPE_THETA = 10000.0
EPS = 1e-6
LN_EPS = 1e-5
ALPHA = (2 * DEPTH) ** 0.25

GDN_H, GDN_DK, GDN_DV = 8, 128, 128
SWA_H, SWA_KVH, SWA_HD, SWA_W = 16, 2, 64, 128
FOX_H, FOX_HD = 16, 64
ML_H, ML_DQK, ML_DV = 8, 64, 128
CHUNK = 64


def _mm(a, b):
    return jnp.dot(a.astype(BF16), b.astype(BF16), preferred_element_type=F32)


def _mm_nt(a, b):
    return lax.dot_general(a.astype(BF16), b.astype(BF16), (((1,), (1,)), ((), ())),
                           preferred_element_type=F32)


def _mm_tn(a, b):
    return lax.dot_general(a.astype(BF16), b.astype(BF16), (((0,), (0,)), ((), ())),
                           preferred_element_type=F32)


def _mm_f32(a, b):
    return jnp.dot(a, b, preferred_element_type=F32, precision=lax.Precision.HIGHEST)


def _mm_nt_f32(a, b):
    return lax.dot_general(a, b, (((1,), (1,)), ((), ())), preferred_element_type=F32,
                           precision=lax.Precision.HIGHEST)


def _sigmoid(x):
    return 1.0 / (1.0 + jnp.exp(-x))


def _silu(x):
    return x * _sigmoid(x)


def _softplus(x):
    return jnp.maximum(x, 0.0) + jnp.log1p(jnp.exp(-jnp.abs(x)))


def _log_sigmoid(x):
    return -_softplus(-x)


def _iota(shape, dim):
    return lax.broadcasted_iota(jnp.int32, shape, dim)


def _tri(n):
    return (_iota((n, n), 0) >= _iota((n, n), 1)).astype(F32)


def _eye(n, m):
    return (_iota((n, m), 0) == _iota((n, m), 1)).astype(F32)


def _conv_silu(x, halo, w):
    n = x.shape[0]
    xx = jnp.concatenate([halo, x], axis=0)
    y = w[CONV_K - 1:CONV_K] * x
    for j in range(1, CONV_K):
        y = y + w[CONV_K - 1 - j:CONV_K - j] * pltpu.roll(xx, j, 0)[8:8 + n]
    return _silu(y)


def _cparams(sem):
    return pltpu.CompilerParams(dimension_semantics=sem, vmem_limit_bytes=VMEM_LIMIT)


def _pad_cols(w, mult=LANES):
    pad = (-w.shape[1]) % mult
    return jnp.pad(w, ((0, 0), (0, pad))) if pad else w


def _lane_vec(vals, offset):
    return jnp.zeros((1, LANES), F32).at[0, offset:offset + vals.shape[0]].set(vals.astype(F32))


def _proj_kernel(x_ref, w_ref, o_ref):
    o_ref[...] = _mm(x_ref[...], w_ref[...])


def _col_tile(m):
    units = m // LANES
    best = 1
    for d in range(1, units + 1):
        if units % d == 0 and d * LANES <= 1536:
            best = d
    return best * LANES


def _in_proj(x2, w):
    n, d = x2.shape
    m = w.shape[1]
    tm, tn = 512, _col_tile(m)
    return pl.pallas_call(
        _proj_kernel,
        grid=(n // tm, m // tn),
        in_specs=[pl.BlockSpec((tm, d), lambda i, j: (i, 0)),
                  pl.BlockSpec((d, tn), lambda i, j: (0, j))],
        out_specs=pl.BlockSpec((tm, tn), lambda i, j: (i, j)),
        out_shape=jax.ShapeDtypeStruct((n, m), F32),
        compiler_params=_cparams(("parallel", "parallel")),
        name="in_proj",
    )(x2, w)


def _out_ln_kernel(og_ref, x_ref, w_ref, g_ref, b_ref, o_ref):
    y = ALPHA * x_ref[...] + _mm(og_ref[...], w_ref[...])
    mu = jnp.mean(y, axis=-1, keepdims=True)
    yc = y - mu
    var = jnp.mean(yc * yc, axis=-1, keepdims=True)
    o_ref[...] = yc * lax.rsqrt(var + LN_EPS) * g_ref[...] + b_ref[...]


def _out_ln(og, x2, w, g, b):
    n, d = x2.shape
    wd = og.shape[1]
    tm = 512
    return pl.pallas_call(
        _out_ln_kernel,
        grid=(n // tm,),
        in_specs=[pl.BlockSpec((tm, wd), lambda i: (i, 0)),
                  pl.BlockSpec((tm, d), lambda i: (i, 0)),
                  pl.BlockSpec((wd, d), lambda i: (0, 0)),
                  pl.BlockSpec((1, d), lambda i: (0, 0)),
                  pl.BlockSpec((1, d), lambda i: (0, 0))],
        out_specs=pl.BlockSpec((tm, d), lambda i: (i, 0)),
        out_shape=jax.ShapeDtypeStruct((n, d), F32),
        compiler_params=_cparams(("parallel",)),
        name="out_proj_ln",
    )(og, x2, w, g.reshape(1, d).astype(F32), b.reshape(1, d).astype(F32))


def _gdn_kernel(qkv_ref, halo_ref, z_ref, gat_ref, cw_ref, alog_ref, dtb_ref, ng_ref,
                o_ref, s_ref):
    j = pl.program_id(1)
    c = CHUNK

    @pl.when(j == 0)
    def _():
        s_ref[...] = jnp.zeros_like(s_ref)

    halo = jnp.where(j > 0, halo_ref[...], 0.0)
    y = _conv_silu(qkv_ref[...], halo, cw_ref[...])

    gat = gat_ref[...]
    beta = _sigmoid(gat)
    g = -jnp.exp(alog_ref[...]) * _softplus(gat + dtb_ref[...])
    gc = _mm_f32(_tri(c), g)
    gct = _mm_nt_f32(_eye(LANES, LANES), gc)

    row = _iota((c, c), 0)
    col = _iota((c, c), 1)
    causal = row >= col
    strict = row > col
    ng = ng_ref[...]
    qoff, koff, voff = 0, GDN_H * GDN_DK, 2 * GDN_H * GDN_DK
    for h in range(GDN_H):
        q = y[:, qoff + h * 128:qoff + (h + 1) * 128]
        k = y[:, koff + h * 128:koff + (h + 1) * 128]
        v = y[:, voff + h * 128:voff + (h + 1) * 128]
        q = q * (lax.rsqrt(jnp.sum(q * q, axis=-1, keepdims=True) + EPS) * GDN_DK ** -0.5)
        k = k * lax.rsqrt(jnp.sum(k * k, axis=-1, keepdims=True) + EPS)
        beta_c = beta[:, h:h + 1]
        gc_c = jnp.broadcast_to(gc[:, 8 + h:9 + h], (c, LANES))
        gc_r = gct[8 + h:9 + h, :]
        decay = jnp.exp(jnp.where(causal, gc_c[:, :c] - gc_r, -jnp.inf))
        kb = k * beta_c
        egc = jnp.exp(gc_c)
        lneg = jnp.where(strict, -(_mm_nt(kb, k) * decay), 0.0)
        rhs = jnp.concatenate([v * beta_c, kb * egc], axis=1)
        p = lneg
        m = lneg
        for _ in range(int(math.log2(c)) - 1):
            m = _mm(m, m)
            p = p + m + _mm(p, m)
        uw = rhs + _mm(p, rhs)
        u = uw[:, :128]
        w = uw[:, 128:]
        qk = jnp.where(causal, _mm_nt(q, k) * decay, 0.0)
        g_last = gc_c[c - 1:c, :]
        q_dec = q * egc
        k_dec = k * jnp.exp(g_last - gc_c)
        s = s_ref[h]
        v_new = u - _mm(w, s)
        o = _mm(q_dec, s) + _mm(qk, v_new)
        s_ref[h] = s * jnp.exp(g_last) + _mm_tn(k_dec, v_new)
        o = o * lax.rsqrt(jnp.mean(o * o, axis=-1, keepdims=True) + EPS) * ng
        zt = z_ref[:, h * 128:(h + 1) * 128]
        o_ref[:, h * 128:(h + 1) * 128] = (o * _silu(zt)).astype(o_ref.dtype)


def _gdn_core(p, bsz, t_len, conv_w, a_log, dt_bias, norm_g):
    n = p.shape[0]
    c = CHUNK
    nc = t_len // c
    cw = 3 * GDN_H * 128
    hb = c // 8
    return pl.pallas_call(
        _gdn_kernel,
        grid=(bsz, nc),
        in_specs=[
            pl.BlockSpec((c, cw), lambda b, j: (b * nc + j, 0)),
            pl.BlockSpec((8, cw), lambda b, j: (jnp.maximum((b * nc + j) * hb - 1, 0), 0)),
            pl.BlockSpec((c, 1024), lambda b, j: (b * nc + j, 3)),
            pl.BlockSpec((c, LANES), lambda b, j: (b * nc + j, 32)),
            pl.BlockSpec((CONV_K, cw), lambda b, j: (0, 0)),
            pl.BlockSpec((1, LANES), lambda b, j: (0, 0)),
            pl.BlockSpec((1, LANES), lambda b, j: (0, 0)),
            pl.BlockSpec((1, LANES), lambda b, j: (0, 0)),
        ],
        out_specs=pl.BlockSpec((c, 1024), lambda b, j: (b * nc + j, 0)),
        out_shape=jax.ShapeDtypeStruct((n, 1024), BF16),
        scratch_shapes=[pltpu.VMEM((GDN_H, GDN_DK, GDN_DV), F32)],
        compiler_params=_cparams(("arbitrary", "arbitrary")),
        name="gdn_core",
    )(p, p, p, p, conv_w.astype(F32), _lane_vec(a_log, 8), _lane_vec(dt_bias, 8),
      norm_g.reshape(1, GDN_DV).astype(F32))


def _rope_tile(x, cos, sin_signed, first_half):
    rot = jnp.where(first_half, pltpu.roll(x, LANES - SWA_HD // 2, 1), pltpu.roll(x, SWA_HD // 2, 1))
    return x * cos + rot * sin_signed


def _swa_kernel(sink_ref, q_ref, z_ref, k_ref, v_ref, cos_ref, sin_ref, o_ref, kp_ref, vp_ref):
    j = pl.program_id(1)
    w = SWA_W

    @pl.when(j == 0)
    def _():
        kp_ref[...] = jnp.zeros_like(kp_ref)
        vp_ref[...] = jnp.zeros_like(vp_ref)

    cos = cos_ref[...]
    sin = sin_ref[...]
    lane = _iota((w, LANES), 1)
    first_half = (lane % SWA_HD) < SWA_HD // 2
    lo = lane < SWA_HD
    lane2 = _iota((2 * w, LANES), 1)
    lo2 = lane2 < SWA_HD

    kr = _rope_tile(k_ref[...], cos, sin, first_half)
    vc = v_ref[...]
    kcat = jnp.concatenate([kp_ref[...], kr], axis=0)
    vcat = jnp.concatenate([vp_ref[...], vc], axis=0)
    kp_ref[...] = kr
    vp_ref[...] = vc
    kswap = pltpu.roll(kcat, SWA_HD, 1)
    vswap = pltpu.roll(vcat, SWA_HD, 1)

    r = _iota((2 * w, 2 * w), 0) % w
    cidx = _iota((2 * w, 2 * w), 1)
    valid2 = ((cidx < w) & (cidx > r) & (j > 0)) | ((cidx >= w) & (cidx - w <= r))
    top = _iota((2 * w, 1), 0) < w

    group = SWA_H // SWA_KVH
    for kvh in range(SWA_KVH):
        own = lo2 if kvh == 0 else jnp.logical_not(lo2)
        kdup = jnp.where(own, kcat, kswap)
        vdup = jnp.where(own, vcat, vswap)
        for t in range(group // 2):
            tile = kvh * (group // 2) + t
            qt = _rope_tile(q_ref[:, tile * LANES:(tile + 1) * LANES], cos, sin, first_half)
            qt = qt * SWA_HD ** -0.5
            q2 = jnp.concatenate([jnp.where(lo, qt, 0.0), jnp.where(lo, 0.0, qt)], axis=0)
            s = jnp.where(valid2, _mm_nt(q2, kdup), -jnp.inf)
            sk = jnp.where(top, sink_ref[2 * tile], sink_ref[2 * tile + 1])
            m = jnp.maximum(jnp.max(s, axis=-1, keepdims=True), sk)
            e = jnp.exp(s - m)
            den = jnp.sum(e, axis=-1, keepdims=True) + jnp.exp(sk - m)
            o2 = _mm(e / den, vdup)
            ot = jnp.where(lo, o2[:w], o2[w:])
            zt = z_ref[:, tile * LANES:(tile + 1) * LANES]
            o_ref[:, tile * LANES:(tile + 1) * LANES] = (ot * _silu(zt)).astype(o_ref.dtype)


def _swa_core(p, bsz, t_len, sinks, cos_t, sin_t):
    n = p.shape[0]
    w = SWA_W
    nb = t_len // w
    q_w = SWA_H * SWA_HD
    return pl.pallas_call(
        _swa_kernel,
        grid=(bsz, nb),
        in_specs=[
            pl.BlockSpec(memory_space=pltpu.SMEM),
            pl.BlockSpec((w, q_w), lambda b, j: (b * nb + j, 0)),
            pl.BlockSpec((w, q_w), lambda b, j: (b * nb + j, 1)),
            pl.BlockSpec((w, LANES), lambda b, j: (b * nb + j, 2 * q_w // LANES)),
            pl.BlockSpec((w, LANES), lambda b, j: (b * nb + j, 2 * q_w // LANES + 1)),
            pl.BlockSpec((w, LANES), lambda b, j: (j, 0)),
            pl.BlockSpec((w, LANES), lambda b, j: (j, 0)),
        ],
        out_specs=pl.BlockSpec((w, q_w), lambda b, j: (b * nb + j, 0)),
        out_shape=jax.ShapeDtypeStruct((n, q_w), BF16),
        scratch_shapes=[pltpu.VMEM((w, LANES), F32), pltpu.VMEM((w, LANES), F32)],
        compiler_params=_cparams(("arbitrary", "arbitrary")),
        name="swa_core",
    )(sinks.astype(F32), p, p, p, p, cos_t, sin_t)


def _rope_tables(t_len):
    half = SWA_HD // 2
    inv = ROPE_THETA ** (-jnp.arange(half, dtype=F32) / half)
    ang = jnp.arange(t_len, dtype=F32)[:, None] * inv[None, :]
    cos, sin = jnp.cos(ang), jnp.sin(ang)
    return (jnp.concatenate([cos, cos, cos, cos], axis=1),
            jnp.concatenate([-sin, sin, -sin, sin], axis=1))


FOX_BIAS_LANES = 3


def _split3(x):
    x1 = x.astype(BF16).astype(F32)
    r1 = x - x1
    x2 = r1.astype(BF16).astype(F32)
    x3 = (r1 - x2).astype(BF16).astype(F32)
    return x1, x2, x3


def _fox_prep_kernel(q_ref, k_ref, v_ref, f_ref, fb_ref, qg_ref, kg_ref,
                     qa_ref, ka_ref, vb_ref, carry_ref):
    j = pl.program_id(1)
    tb = q_ref.shape[0]

    @pl.when(j == 0)
    def _():
        carry_ref[...] = jnp.zeros_like(carry_ref)

    logf = _log_sigmoid(f_ref[...] + fb_ref[...])
    cum = _mm_f32(_tri(tb), logf) + carry_ref[...]
    carry_ref[...] = cum[tb - 1:tb, :]

    lane = _iota((tb, LANES), 1)
    ones2 = ((_iota((LANES, LANES), 0) // FOX_HD) == (_iota((LANES, LANES), 1) // FOX_HD)).astype(BF16)

    def norm(x, g):
        x2 = x * x
        hi = x2.astype(BF16)
        lo = (x2 - hi.astype(F32)).astype(BF16)
        ss = (jnp.dot(hi, ones2, preferred_element_type=F32)
              + jnp.dot(lo, ones2, preferred_element_type=F32))
        return x * lax.rsqrt(ss * (1.0 / FOX_HD) + EPS) * g

    head_lane = lane < FOX_HD
    bl = lane - FOX_HD
    for t in range(FOX_H // 2):
        sl = slice(t * LANES, (t + 1) * LANES)
        qn = norm(q_ref[:, sl], qg_ref[...]) * FOX_HD ** -0.5
        kn = norm(k_ref[:, sl], kg_ref[...])
        vb_ref[:, sl] = v_ref[:, sl].astype(BF16)
        for half in range(2):
            h = 2 * t + half
            c1, c2, c3 = _split3(jnp.broadcast_to(cum[:, h:h + 1], (tb, LANES)))
            qb = jnp.where(bl == 0, c1, jnp.where(bl == 1, c2, jnp.where(bl == 2, c3,
                 jnp.where((bl >= 3) & (bl < 6), 1.0, 0.0))))
            kb = jnp.where(bl == 3, -c1, jnp.where(bl == 4, -c2, jnp.where(bl == 5, -c3,
                 jnp.where((bl >= 0) & (bl < 3), 1.0, 0.0))))
            qh = qn if half == 0 else pltpu.roll(qn, FOX_HD, 1)
            kh = kn if half == 0 else pltpu.roll(kn, FOX_HD, 1)
            qa_ref[:, h * LANES:(h + 1) * LANES] = jnp.where(head_lane, qh, qb).astype(BF16)
            ka_ref[:, h * LANES:(h + 1) * LANES] = jnp.where(head_lane, kh, kb).astype(BF16)


def _fox_prep(p, bsz, t_len, f_bias, q_norm_g, k_norm_g):
    n = p.shape[0]
    tb = 128
    nb = t_len // tb
    wd = FOX_H * FOX_HD
    row = lambda b, j: b * nb + j
    g2 = lambda g: jnp.concatenate([g, g]).reshape(1, LANES).astype(F32)
    return pl.pallas_call(
        _fox_prep_kernel,
        grid=(bsz, nb),
        in_specs=[
            pl.BlockSpec((tb, wd), lambda b, j: (row(b, j), 0)),
            pl.BlockSpec((tb, wd), lambda b, j: (row(b, j), 1)),
            pl.BlockSpec((tb, wd), lambda b, j: (row(b, j), 2)),
            pl.BlockSpec((tb, LANES), lambda b, j: (row(b, j), 4 * wd // LANES)),
            pl.BlockSpec((1, LANES), lambda b, j: (0, 0)),
            pl.BlockSpec((1, LANES), lambda b, j: (0, 0)),
            pl.BlockSpec((1, LANES), lambda b, j: (0, 0)),
        ],
        out_specs=[
            pl.BlockSpec((tb, 2 * wd), lambda b, j: (row(b, j), 0)),
            pl.BlockSpec((tb, 2 * wd), lambda b, j: (row(b, j), 0)),
            pl.BlockSpec((tb, wd), lambda b, j: (row(b, j), 0)),
        ],
        out_shape=[jax.ShapeDtypeStruct((n, 2 * wd), BF16),
                   jax.ShapeDtypeStruct((n, 2 * wd), BF16),
                   jax.ShapeDtypeStruct((n, wd), BF16)],
        scratch_shapes=[pltpu.VMEM((1, LANES), F32)],
        compiler_params=_cparams(("arbitrary", "arbitrary")),
        name="fox_prep",
    )(p, p, p, p, _lane_vec(f_bias, 0), g2(q_norm_g), g2(k_norm_g))


FOX_TQ = 512


def _fox_flash_kernel(qa_ref, ka_ref, v_ref, z_ref, o_ref):
    i = pl.program_id(2)
    tq = FOX_TQ
    lane = _iota((tq, LANES), 1)
    lo = lane < FOX_HD
    diag = _iota((tq, tq), 0) >= _iota((tq, tq), 1)
    q = [qa_ref[:, hh * LANES:(hh + 1) * LANES] for hh in range(2)]

    def step(jb, carry, masked):
        start = pl.multiple_of(jb * tq, tq)
        vt = v_ref[pl.ds(start, tq), :]
        out = []
        for hh in range(2):
            m_prev, l_prev, acc = carry[hh]
            kt = ka_ref[pl.ds(start, tq), hh * LANES:(hh + 1) * LANES]
            s = lax.dot_general(q[hh], kt, (((1,), (1,)), ((), ())), preferred_element_type=F32)
            if masked:
                s = jnp.where(diag, s, -jnp.inf)
            m_new = jnp.maximum(m_prev, jnp.max(s, axis=-1, keepdims=True))
            a = jnp.exp(m_prev - m_new)
            e = jnp.exp(s - m_new)
            l_new = a * l_prev + jnp.sum(e, axis=-1, keepdims=True)
            acc = a * acc + jnp.dot(e.astype(BF16), vt, preferred_element_type=F32)
            out.append((m_new, l_new, acc))
        return tuple(out)

    init = tuple((jnp.full((tq, 1), -jnp.inf, F32), jnp.zeros((tq, 1), F32),
                  jnp.zeros((tq, LANES), F32)) for _ in range(2))
    carry = lax.fori_loop(0, i, lambda jb, cr: step(jb, cr, False), init)
    (_, l0, a0), (_, l1, a1) = step(i, carry, True)
    o = jnp.where(lo, a0 / l0, a1 / l1)
    o_ref[...] = (o * _silu(z_ref[...])).astype(o_ref.dtype)


def _fox_flash(qa, ka, vb, p, bsz, t_len):
    n = qa.shape[0]
    tq = FOX_TQ
    nq = t_len // tq
    wd = FOX_H * FOX_HD
    return pl.pallas_call(
        _fox_flash_kernel,
        grid=(bsz, FOX_H // 2, nq),
        in_specs=[
            pl.BlockSpec((tq, 2 * LANES), lambda b, hp, i: (b * nq + i, hp)),
            pl.BlockSpec((t_len, 2 * LANES), lambda b, hp, i: (b, hp)),
            pl.BlockSpec((t_len, LANES), lambda b, hp, i: (b, hp)),
            pl.BlockSpec((tq, LANES), lambda b, hp, i: (b * nq + i, 3 * wd // LANES + hp)),
        ],
        out_specs=pl.BlockSpec((tq, LANES), lambda b, hp, i: (b * nq + i, hp)),
        out_shape=jax.ShapeDtypeStruct((n, wd), BF16),
        compiler_params=_cparams(("parallel", "parallel", "arbitrary")),
        name="fox_flash",
    )(qa, ka, vb, p)


def _mlstm_kernel(qk_ref, halo_ref, v_ref, og_ref, z_ref, gat_ref, cw_ref, gb_ref, ng_ref,
                  o_ref, c_ref, m_ref):
    j = pl.program_id(1)
    c = CHUNK

    @pl.when(j == 0)
    def _():
        c_ref[...] = jnp.zeros_like(c_ref)
        m_ref[...] = jnp.zeros_like(m_ref)

    halo = jnp.where(j > 0, halo_ref[...], 0.0)
    y = _conv_silu(qk_ref[...], halo, cw_ref[...])

    gt = gat_ref[...] + gb_ref[...]
    lane = _iota((c, LANES), 1)
    bsum = _mm_f32(_tri(c), _log_sigmoid(gt))
    rows = _mm_nt_f32(_eye(LANES, LANES), jnp.where(lane < ML_H, gt, bsum))

    causal = _iota((c, c), 0) >= _iota((c, c), 1)
    lo = lane < ML_DQK
    ones_col = (lane == 0).astype(F32)
    ng = ng_ref[...]
    kbase = ML_H * ML_DQK
    for h in range(ML_H):
        t, half = h // 2, h % 2
        own = lo if half == 0 else jnp.logical_not(lo)
        q = jnp.where(own, y[:, t * LANES:(t + 1) * LANES], 0.0) * ML_DQK ** -0.5
        k = jnp.where(own, y[:, kbase + t * LANES:kbase + (t + 1) * LANES], 0.0)
        v = v_ref[:, h * ML_DV:(h + 1) * ML_DV]
        v_aug = jnp.concatenate([v, ones_col], axis=1)

        b_c = jnp.broadcast_to(bsum[:, ML_H + h:ML_H + h + 1], (c, LANES))
        ig_c = jnp.broadcast_to(gt[:, h:h + 1], (c, LANES))
        b_r = rows[ML_H + h:ML_H + h + 1, :]
        ig_r = rows[h:h + 1, :]
        b_last = b_c[c - 1:c, :]
        m_st = m_ref[h]

        d = jnp.where(causal, b_c[:, :c] - b_r + ig_r, -jnp.inf)
        m_intra = jnp.max(d, axis=-1, keepdims=True)
        a_end = b_last - b_c + ig_c
        m_end = jnp.max(a_end, axis=0, keepdims=True)

        inter = b_c + m_st
        m_t = jnp.maximum(inter, m_intra)
        pm = _mm_nt(q, k) * jnp.exp(d - m_t[:, :c])
        w_inter = jnp.exp(inter - m_t)
        w_inter2 = jnp.concatenate([w_inter, w_inter], axis=1)
        num = w_inter2 * _mm(q, c_ref[h]) + _mm(pm, v_aug)
        den = num[:, ML_DV:ML_DV + 1]
        hv = num[:, :ML_DV] / jnp.maximum(jnp.abs(den), jnp.exp(-m_t))

        m_new = jnp.maximum(b_last + m_st, m_end)
        w_old = jnp.exp(b_last + m_st - m_new)
        w_tok = jnp.exp(a_end - m_new)
        c_ref[h] = (jnp.concatenate([w_old, w_old], axis=1) * c_ref[h]
                    + _mm_tn(k * w_tok, v_aug))
        m_ref[h] = m_new

        hv = hv * lax.rsqrt(jnp.mean(hv * hv, axis=-1, keepdims=True) + EPS) * ng
        sl = slice(h * ML_DV, (h + 1) * ML_DV)
        o_ref[:, sl] = (hv * _sigmoid(og_ref[:, sl]) * _silu(z_ref[:, sl])).astype(o_ref.dtype)


def _mlstm_core(p, bsz, t_len, conv_w, gate_bias, norm_g):
    n = p.shape[0]
    c = CHUNK
    nc = t_len // c
    hb = c // 8
    row = lambda b, j: b * nc + j
    return pl.pallas_call(
        _mlstm_kernel,
        grid=(bsz, nc),
        in_specs=[
            pl.BlockSpec((c, 1024), lambda b, j: (row(b, j), 0)),
            pl.BlockSpec((8, 1024), lambda b, j: (jnp.maximum(row(b, j) * hb - 1, 0), 0)),
            pl.BlockSpec((c, 1024), lambda b, j: (row(b, j), 1)),
            pl.BlockSpec((c, 1024), lambda b, j: (row(b, j), 2)),
            pl.BlockSpec((c, 1024), lambda b, j: (row(b, j), 3)),
            pl.BlockSpec((c, LANES), lambda b, j: (row(b, j), 32)),
            pl.BlockSpec((CONV_K, 1024), lambda b, j: (0, 0)),
            pl.BlockSpec((1, LANES), lambda b, j: (0, 0)),
            pl.BlockSpec((1, LANES), lambda b, j: (0, 0)),
        ],
        out_specs=pl.BlockSpec((c, 1024), lambda b, j: (row(b, j), 0)),
        out_shape=jax.ShapeDtypeStruct((n, 1024), BF16),
        scratch_shapes=[pltpu.VMEM((ML_H, LANES, 2 * ML_DV), F32),
                        pltpu.VMEM((ML_H, 1, LANES), F32)],
        compiler_params=_cparams(("arbitrary", "arbitrary")),
        name="mlstm_core",
    )(p, p, p, p, p, p, conv_w.astype(F32), _lane_vec(gate_bias, 0),
      norm_g.reshape(1, ML_DV).astype(F32))


def _gdn_layer(x2, bsz, t_len, w_in, conv_w, a_log, dt_bias, norm_g, w_out, ln_g, ln_b):
    p = _in_proj(x2, _pad_cols(w_in).astype(BF16))
    og = _gdn_core(p, bsz, t_len, conv_w, a_log, dt_bias, norm_g)
    return _out_ln(og, x2, w_out.astype(BF16), ln_g, ln_b)


def _swa_layer(x2, bsz, t_len, w_in, sinks, w_out, ln_g, ln_b):
    q_w, kv_w = SWA_H * SWA_HD, SWA_KVH * SWA_HD
    w_perm = jnp.concatenate([w_in[:, :q_w], w_in[:, q_w + 2 * kv_w:],
                              w_in[:, q_w:q_w + 2 * kv_w]], axis=1)
    p = _in_proj(x2, w_perm.astype(BF16))
    cos_t, sin_t = _rope_tables(t_len)
    og = _swa_core(p, bsz, t_len, sinks, cos_t, sin_t)
    return _out_ln(og, x2, w_out.astype(BF16), ln_g, ln_b)


def _fox_layer(x2, bsz, t_len, w_in, f_bias, q_norm_g, k_norm_g, w_out, ln_g, ln_b):
    p = _in_proj(x2, _pad_cols(w_in).astype(BF16))
    qa, ka, vb = _fox_prep(p, bsz, t_len, f_bias, q_norm_g, k_norm_g)
    og = _fox_flash(qa, ka, vb, p, bsz, t_len)
    return _out_ln(og, x2, w_out.astype(BF16), ln_g, ln_b)


def _mlstm_layer(x2, bsz, t_len, w_in, conv_w, gate_bias, norm_g, w_out, ln_g, ln_b):
    p = _in_proj(x2, _pad_cols(w_in).astype(BF16))
    og = _mlstm_core(p, bsz, t_len, conv_w, gate_bias, norm_g)
    return _out_ln(og, x2, w_out.astype(BF16), ln_g, ln_b)


def kernel(x, a_w_in, a_conv, a_a_log, a_dt_bias, a_norm_g, a_w_out, b_w_in, b_sinks, b_w_out, c_w_in, c_f_bias, c_q_norm, c_k_norm, c_w_out, d_w_in, d_conv, d_gate_bias, d_norm_g, d_w_out, ln_g, ln_b):
    bsz, t_len, d = x.shape
    x2 = x.reshape(bsz * t_len, d)
    for i in range(DEPTH):
        kind, j = i % 4, i // 4
        if kind == 0:
            x2 = _gdn_layer(x2, bsz, t_len, a_w_in[j], a_conv[j], a_a_log[j], a_dt_bias[j],
                            a_norm_g[j], a_w_out[j], ln_g[i], ln_b[i])
        elif kind == 1:
            x2 = _swa_layer(x2, bsz, t_len, b_w_in[j], b_sinks[j], b_w_out[j], ln_g[i], ln_b[i])
        elif kind == 2:
            x2 = _fox_layer(x2, bsz, t_len, c_w_in[j], c_f_bias[j], c_q_norm[j], c_k_norm[j],
                            c_w_out[j], ln_g[i], ln_b[i])
        else:
            x2 = _mlstm_layer(x2, bsz, t_len, d_w_in[j], d_conv[j], d_gate_bias[j], d_norm_g[j],
                              d_w_out[j], ln_g[i], ln_b[i])
    return x2.reshape(bsz, t_len, d)
```

```python
import functools
import math

import jax
import jax.numpy as jnp
from jax import lax
from jax.experimental import pallas as pl
from jax.experimental.pallas import tpu as pltpu

F32 = jnp.float32
BF16 = jnp.bfloat16
LANES = 128
VMEM_LIMIT = 56 * 1024 * 1024

D_MODEL = 1024
DEPTH = 4
CONV_K = 4
ROPE_THETA = 10000.0
EPS = 1e-6
LN_EPS = 1e-5
ALPHA = (2 * DEPTH) ** 0.25

GDN_H, GDN_DK, GDN_DV = 8, 128, 128
SWA_H, SWA_KVH, SWA_HD, SWA_W = 16, 2, 64, 128
FOX_H, FOX_HD = 16, 64
ML_H, ML_DQK, ML_DV = 8, 64, 128
CHUNK = 64


def _mm(a, b):
    return jnp.dot(a.astype(BF16), b.astype(BF16), preferred_element_type=F32)


def _mm_nt(a, b):
    return lax.dot_general(a.astype(BF16), b.astype(BF16), (((1,), (1,)), ((), ())),
                           preferred_element_type=F32)


def _mm_tn(a, b):
    return lax.dot_general(a.astype(BF16), b.astype(BF16), (((0,), (0,)), ((), ())),
                           preferred_element_type=F32)


def _mm_f32(a, b):
    return jnp.dot(a, b, preferred_element_type=F32, precision=lax.Precision.HIGHEST)


def _mm_nt_f32(a, b):
    return lax.dot_general(a, b, (((1,), (1,)), ((), ())), preferred_element_type=F32,
                           precision=lax.Precision.HIGHEST)


def _sigmoid(x):
    return 1.0 / (1.0 + jnp.exp(-x))


def _silu(x):
    return x * _sigmoid(x)


def _softplus(x):
    return jnp.maximum(x, 0.0) + jnp.log1p(jnp.exp(-jnp.abs(x)))


def _log_sigmoid(x):
    return -_softplus(-x)


def _iota(shape, dim):
    return lax.broadcasted_iota(jnp.int32, shape, dim)


def _tri(n):
    return (_iota((n, n), 0) >= _iota((n, n), 1)).astype(F32)


def _eye(n, m):
    return (_iota((n, m), 0) == _iota((n, m), 1)).astype(F32)


def _conv_silu(x, halo, w):
    n = x.shape[0]
    xx = jnp.concatenate([halo, x], axis=0)
    y = w[CONV_K - 1:CONV_K] * x
    for j in range(1, CONV_K):
        y = y + w[CONV_K - 1 - j:CONV_K - j] * pltpu.roll(xx, j, 0)[8:8 + n]
    return _silu(y)


def _cparams(sem):
    return pltpu.CompilerParams(dimension_semantics=sem, vmem_limit_bytes=VMEM_LIMIT)


def _pad_cols(w, mult=LANES):
    pad = (-w.shape[1]) % mult
    return jnp.pad(w, ((0, 0), (0, pad))) if pad else w


def _lane_vec(vals, offset):
    return jnp.zeros((1, LANES), F32).at[0, offset:offset + vals.shape[0]].set(vals.astype(F32))


def _proj_kernel(x_ref, w_ref, o_ref):
    o_ref[...] = _mm(x_ref[...], w_ref[...])


def _in_proj(x2, w):
    n, d = x2.shape
    m = w.shape[1]
    tm = 512
    return pl.pallas_call(
        _proj_kernel,
        grid=(n // tm,),
        in_specs=[pl.BlockSpec((tm, d), lambda i: (i, 0)),
                  pl.BlockSpec((d, m), lambda i: (0, 0))],
        out_specs=pl.BlockSpec((tm, m), lambda i: (i, 0)),
        out_shape=jax.ShapeDtypeStruct((n, m), F32),
        compiler_params=_cparams(("parallel",)),
        name="in_proj",
    )(x2, w)


def _out_ln_kernel(og_ref, x_ref, w_ref, g_ref, b_ref, o_ref):
    y = ALPHA * x_ref[...] + _mm(og_ref[...], w_ref[...])
    mu = jnp.mean(y, axis=-1, keepdims=True)
    yc = y - mu
    var = jnp.mean(yc * yc, axis=-1, keepdims=True)
    o_ref[...] = yc * lax.rsqrt(var + LN_EPS) * g_ref[...] + b_ref[...]


def _out_ln(og, x2, w, g, b):
    n, d = x2.shape
    wd = og.shape[1]
    tm = 512
    return pl.pallas_call(
        _out_ln_kernel,
        grid=(n // tm,),
        in_specs=[pl.BlockSpec((tm, wd), lambda i: (i, 0)),
                  pl.BlockSpec((tm, d), lambda i: (i, 0)),
                  pl.BlockSpec((wd, d), lambda i: (0, 0)),
                  pl.BlockSpec((1, d), lambda i: (0, 0)),
                  pl.BlockSpec((1, d), lambda i: (0, 0))],
        out_specs=pl.BlockSpec((tm, d), lambda i: (i, 0)),
        out_shape=jax.ShapeDtypeStruct((n, d), F32),
        compiler_params=_cparams(("parallel",)),
        name="out_proj_ln",
    )(og, x2, w, g.reshape(1, d).astype(F32), b.reshape(1, d).astype(F32))


GDN_TB = 256


def _gdn_kernel(qkv_ref, halo_ref, z_ref, gat_ref, cw_ref, alog_ref, dtb_ref, ng_ref,
                o_ref, s_ref):
    j = pl.program_id(1)
    c, tb = CHUNK, GDN_TB
    nch = tb // c
    probs = [(ci, h) for ci in range(nch) for h in range(GDN_H)]

    @pl.when(j == 0)
    def _():
        s_ref[...] = jnp.zeros_like(s_ref)

    halo = jnp.where(j > 0, halo_ref[...], 0.0)
    y = _conv_silu(qkv_ref[...], halo, cw_ref[...])

    gat = gat_ref[...]
    beta = _sigmoid(gat)
    g = -jnp.exp(alog_ref[...]) * _softplus(gat + dtb_ref[...])
    rt, ct = _iota((tb, tb), 0), _iota((tb, tb), 1)
    tri_bd = ((rt >= ct) & (rt // c == ct // c)).astype(F32)
    gc = _mm_f32(tri_bd, g)
    gct = _mm_nt_f32(_eye(LANES, LANES), gc)

    row = _iota((c, c), 0)
    col = _iota((c, c), 1)
    causal = row >= col
    strict = row > col
    ng = ng_ref[...]
    qoff, koff, voff = 0, GDN_H * GDN_DK, 2 * GDN_H * GDN_DK

    kn, kb, qn, qd, kd, rhs, gl = {}, {}, {}, {}, {}, {}, {}
    for h in range(GDN_H):
        q = y[:, qoff + h * 128:qoff + (h + 1) * 128]
        k = y[:, koff + h * 128:koff + (h + 1) * 128]
        v = y[:, voff + h * 128:voff + (h + 1) * 128]
        q = q * (lax.rsqrt(jnp.sum(q * q, axis=-1, keepdims=True) + EPS) * GDN_DK ** -0.5)
        k = k * lax.rsqrt(jnp.sum(k * k, axis=-1, keepdims=True) + EPS)
        beta_c = beta[:, h:h + 1]
        gc_c = jnp.broadcast_to(gc[:, 8 + h:9 + h], (tb, LANES))
        egc = jnp.exp(gc_c)
        g_last = jnp.concatenate(
            [jnp.broadcast_to(gc_c[(ci + 1) * c - 1:(ci + 1) * c, :], (c, LANES))
             for ci in range(nch)], axis=0)
        kbh = k * beta_c
        kn[h], kb[h], qn[h] = k, kbh, q
        qd[h] = q * egc
        kd[h] = k * jnp.exp(g_last - gc_c)
        rhs[h] = jnp.concatenate([v * beta_c, kbh * egc], axis=1)
        gl[h] = g_last

    def rows(a, ci):
        return a[ci * c:(ci + 1) * c]

    neg_l, qk = {}, {}
    for ci, h in probs:
        gmat = _mm_nt(jnp.concatenate([rows(kb[h], ci), rows(qn[h], ci)], axis=0), rows(kn[h], ci))
        gc_c = jnp.broadcast_to(gc[ci * c:(ci + 1) * c, 8 + h:9 + h], (c, c))
        gc_r = gct[8 + h:9 + h, ci * c:(ci + 1) * c]
        decay = jnp.exp(jnp.where(causal, gc_c - gc_r, -jnp.inf))
        neg_l[ci, h] = jnp.where(strict, -(gmat[:c] * decay), 0.0)
        qk[ci, h] = jnp.where(causal, gmat[c:] * decay, 0.0)
    p = dict(neg_l)
    m = {key: _mm(neg_l[key], neg_l[key]) for key in probs}
    nsq = int(math.log2(c)) - 1
    for it in range(nsq):
        if it < nsq - 1:
            r = {key: _mm(jnp.concatenate([p[key], m[key]], axis=0), m[key]) for key in probs}
            p = {key: p[key] + m[key] + r[key][:c] for key in probs}
            m = {key: r[key][c:] for key in probs}
        else:
            p = {key: p[key] + m[key] + _mm(p[key], m[key]) for key in probs}
    uw = {}
    for ci, h in probs:
        rh = rows(rhs[h], ci)
        uw[ci, h] = rh + _mm(p[ci, h], rh)

    s = [s_ref[h] for h in range(GDN_H)]
    for ci in range(nch):
        t = [_mm(jnp.concatenate([uw[ci, h][:, 128:], rows(qd[h], ci)], axis=0), s[h])
             for h in range(GDN_H)]
        v_new = [uw[ci, h][:, :128] - t[h][:c] for h in range(GDN_H)]
        o = [t[h][c:] + _mm(qk[ci, h], v_new[h]) for h in range(GDN_H)]
        s = [s[h] * jnp.exp(gl[h][ci * c:ci * c + 1, :]) + _mm_tn(rows(kd[h], ci), v_new[h])
             for h in range(GDN_H)]
        for h in range(GDN_H):
            oh = o[h] * lax.rsqrt(jnp.mean(o[h] * o[h], axis=-1, keepdims=True) + EPS) * ng
            zt = z_ref[ci * c:(ci + 1) * c, h * 128:(h + 1) * 128]
            o_ref[ci * c:(ci + 1) * c, h * 128:(h + 1) * 128] = (oh * _silu(zt)).astype(o_ref.dtype)
    for h in range(GDN_H):
        s_ref[h] = s[h]


def _gdn_core(p, bsz, t_len, conv_w, a_log, dt_bias, norm_g):
    n = p.shape[0]
    tb = GDN_TB
    nb = t_len // tb
    cw = 3 * GDN_H * 128
    hb = tb // 8
    return pl.pallas_call(
        _gdn_kernel,
        grid=(bsz, nb),
        in_specs=[
            pl.BlockSpec((tb, cw), lambda b, j: (b * nb + j, 0)),
            pl.BlockSpec((8, cw), lambda b, j: (jnp.maximum((b * nb + j) * hb - 1, 0), 0)),
            pl.BlockSpec((tb, 1024), lambda b, j: (b * nb + j, 3)),
            pl.BlockSpec((tb, LANES), lambda b, j: (b * nb + j, 32)),
            pl.BlockSpec((CONV_K, cw), lambda b, j: (0, 0)),
            pl.BlockSpec((1, LANES), lambda b, j: (0, 0)),
            pl.BlockSpec((1, LANES), lambda b, j: (0, 0)),
            pl.BlockSpec((1, LANES), lambda b, j: (0, 0)),
        ],
        out_specs=pl.BlockSpec((tb, 1024), lambda b, j: (b * nb + j, 0)),
        out_shape=jax.ShapeDtypeStruct((n, 1024), BF16),
        scratch_shapes=[pltpu.VMEM((GDN_H, GDN_DK, GDN_DV), F32)],
        compiler_params=_cparams(("arbitrary", "arbitrary")),
        name="gdn_core",
    )(p, p, p, p, conv_w.astype(F32), _lane_vec(a_log, 8), _lane_vec(dt_bias, 8),
      norm_g.reshape(1, GDN_DV).astype(F32))


def _rope_tile(x, cos, sin_signed, first_half):
    rot = jnp.where(first_half, pltpu.roll(x, LANES - SWA_HD // 2, 1), pltpu.roll(x, SWA_HD // 2, 1))
    return x * cos + rot * sin_signed


def _swa_kernel(sink_ref, q_ref, z_ref, k_ref, v_ref, cos_ref, sin_ref, o_ref, kp_ref, vp_ref):
    j = pl.program_id(1)
    w = SWA_W

    @pl.when(j == 0)
    def _():
        kp_ref[...] = jnp.zeros_like(kp_ref)
        vp_ref[...] = jnp.zeros_like(vp_ref)

    cos = cos_ref[...]
    sin = sin_ref[...]
    lane = _iota((w, LANES), 1)
    first_half = (lane % SWA_HD) < SWA_HD // 2
    lo = lane < SWA_HD
    lane2 = _iota((2 * w, LANES), 1)
    lo2 = lane2 < SWA_HD

    kr = _rope_tile(k_ref[...], cos, sin, first_half)
    vc = v_ref[...]
    kcat = jnp.concatenate([kp_ref[...], kr], axis=0)
    vcat = jnp.concatenate([vp_ref[...], vc], axis=0)
    kp_ref[...] = kr
    vp_ref[...] = vc
    kswap = pltpu.roll(kcat, SWA_HD, 1)
    vswap = pltpu.roll(vcat, SWA_HD, 1)

    r = _iota((2 * w, 2 * w), 0) % w
    cidx = _iota((2 * w, 2 * w), 1)
    valid2 = ((cidx < w) & (cidx > r) & (j > 0)) | ((cidx >= w) & (cidx - w <= r))
    top = _iota((2 * w, 1), 0) < w

    group = SWA_H // SWA_KVH
    kdup, vdup = [], []
    for kvh in range(SWA_KVH):
        own = lo2 if kvh == 0 else jnp.logical_not(lo2)
        kdup.append(jnp.where(own, kcat, kswap))
        vdup.append(jnp.where(own, vcat, vswap))
    tiles = range(SWA_H // 2)
    scores = []
    for tile in tiles:
        qt = _rope_tile(q_ref[:, tile * LANES:(tile + 1) * LANES], cos, sin, first_half)
        qt = qt * SWA_HD ** -0.5
        q2 = jnp.concatenate([jnp.where(lo, qt, 0.0), jnp.where(lo, 0.0, qt)], axis=0)
        scores.append(jnp.where(valid2, _mm_nt(q2, kdup[tile // (group // 2)]), -jnp.inf))
    probs = []
    for tile in tiles:
        s = scores[tile]
        sk = jnp.where(top, sink_ref[2 * tile], sink_ref[2 * tile + 1])
        m = jnp.maximum(jnp.max(s, axis=-1, keepdims=True), sk)
        e = jnp.exp(s - m)
        den = jnp.sum(e, axis=-1, keepdims=True) + jnp.exp(sk - m)
        probs.append(e / den)
    for tile in tiles:
        o2 = _mm(probs[tile], vdup[tile // (group // 2)])
        ot = jnp.where(lo, o2[:w], o2[w:])
        zt = z_ref[:, tile * LANES:(tile + 1) * LANES]
        o_ref[:, tile * LANES:(tile + 1) * LANES] = (ot * _silu(zt)).astype(o_ref.dtype)


def _swa_core(p, bsz, t_len, sinks, cos_t, sin_t):
    n = p.shape[0]
    w = SWA_W
    nb = t_len // w
    q_w = SWA_H * SWA_HD
    return pl.pallas_call(
        _swa_kernel,
        grid=(bsz, nb),
        in_specs=[
            pl.BlockSpec(memory_space=pltpu.SMEM),
            pl.BlockSpec((w, q_w), lambda b, j: (b * nb + j, 0)),
            pl.BlockSpec((w, q_w), lambda b, j: (b * nb + j, 1)),
            pl.BlockSpec((w, LANES), lambda b, j: (b * nb + j, 2 * q_w // LANES)),
            pl.BlockSpec((w, LANES), lambda b, j: (b * nb + j, 2 * q_w // LANES + 1)),
            pl.BlockSpec((w, LANES), lambda b, j: (j, 0)),
            pl.BlockSpec((w, LANES), lambda b, j: (j, 0)),
        ],
        out_specs=pl.BlockSpec((w, q_w), lambda b, j: (b * nb + j, 0)),
        out_shape=jax.ShapeDtypeStruct((n, q_w), BF16),
        scratch_shapes=[pltpu.VMEM((w, LANES), F32), pltpu.VMEM((w, LANES), F32)],
        compiler_params=_cparams(("arbitrary", "arbitrary")),
        name="swa_core",
    )(sinks.astype(F32), p, p, p, p, cos_t, sin_t)


def _rope_tables(t_len):
    half = SWA_HD // 2
    inv = ROPE_THETA ** (-jnp.arange(half, dtype=F32) / half)
    ang = jnp.arange(t_len, dtype=F32)[:, None] * inv[None, :]
    cos, sin = jnp.cos(ang), jnp.sin(ang)
    return (jnp.concatenate([cos, cos, cos, cos], axis=1),
            jnp.concatenate([-sin, sin, -sin, sin], axis=1))


FOX_BIAS_LANES = 3


def _split3(x):
    x1 = x.astype(BF16).astype(F32)
    r1 = x - x1
    x2 = r1.astype(BF16).astype(F32)
    x3 = (r1 - x2).astype(BF16).astype(F32)
    return x1, x2, x3


def _fox_prep_kernel(q_ref, k_ref, v_ref, f_ref, fb_ref, qg_ref, kg_ref,
                     qa_ref, ka_ref, vb_ref, carry_ref):
    j = pl.program_id(1)
    tb = q_ref.shape[0]

    @pl.when(j == 0)
    def _():
        carry_ref[...] = jnp.zeros_like(carry_ref)

    logf = _log_sigmoid(f_ref[...] + fb_ref[...])
    cum = _mm_f32(_tri(tb), logf) + carry_ref[...]
    carry_ref[...] = cum[tb - 1:tb, :]

    lane = _iota((tb, LANES), 1)
    ones2 = ((_iota((LANES, LANES), 0) // FOX_HD) == (_iota((LANES, LANES), 1) // FOX_HD)).astype(BF16)

    def norm(x, g):
        x2 = x * x
        hi = x2.astype(BF16)
        lo = (x2 - hi.astype(F32)).astype(BF16)
        ss = (jnp.dot(hi, ones2, preferred_element_type=F32)
              + jnp.dot(lo, ones2, preferred_element_type=F32))
        return x * lax.rsqrt(ss * (1.0 / FOX_HD) + EPS) * g

    head_lane = lane < FOX_HD
    bl = lane - FOX_HD
    for t in range(FOX_H // 2):
        sl = slice(t * LANES, (t + 1) * LANES)
        qn = norm(q_ref[:, sl], qg_ref[...]) * FOX_HD ** -0.5
        kn = norm(k_ref[:, sl], kg_ref[...])
        vb_ref[:, sl] = v_ref[:, sl].astype(BF16)
        for half in range(2):
            h = 2 * t + half
            c1, c2, c3 = _split3(jnp.broadcast_to(cum[:, h:h + 1], (tb, LANES)))
            qb = jnp.where(bl == 0, c1, jnp.where(bl == 1, c2, jnp.where(bl == 2, c3,
                 jnp.where((bl >= 3) & (bl < 6), 1.0, 0.0))))
            kb = jnp.where(bl == 3, -c1, jnp.where(bl == 4, -c2, jnp.where(bl == 5, -c3,
                 jnp.where((bl >= 0) & (bl < 3), 1.0, 0.0))))
            qh = qn if half == 0 else pltpu.roll(qn, FOX_HD, 1)
            kh = kn if half == 0 else pltpu.roll(kn, FOX_HD, 1)
            qa_ref[:, h * LANES:(h + 1) * LANES] = jnp.where(head_lane, qh, qb).astype(BF16)
            ka_ref[:, h * LANES:(h + 1) * LANES] = jnp.where(head_lane, kh, kb).astype(BF16)


def _fox_prep(p, bsz, t_len, f_bias, q_norm_g, k_norm_g):
    n = p.shape[0]
    tb = 128
    nb = t_len // tb
    wd = FOX_H * FOX_HD
    row = lambda b, j: b * nb + j
    g2 = lambda g: jnp.concatenate([g, g]).reshape(1, LANES).astype(F32)
    return pl.pallas_call(
        _fox_prep_kernel,
        grid=(bsz, nb),
        in_specs=[
            pl.BlockSpec((tb, wd), lambda b, j: (row(b, j), 0)),
            pl.BlockSpec((tb, wd), lambda b, j: (row(b, j), 1)),
            pl.BlockSpec((tb, wd), lambda b, j: (row(b, j), 2)),
            pl.BlockSpec((tb, LANES), lambda b, j: (row(b, j), 4 * wd // LANES)),
            pl.BlockSpec((1, LANES), lambda b, j: (0, 0)),
            pl.BlockSpec((1, LANES), lambda b, j: (0, 0)),
            pl.BlockSpec((1, LANES), lambda b, j: (0, 0)),
        ],
        out_specs=[
            pl.BlockSpec((tb, 2 * wd), lambda b, j: (row(b, j), 0)),
            pl.BlockSpec((tb, 2 * wd), lambda b, j: (row(b, j), 0)),
            pl.BlockSpec((tb, wd), lambda b, j: (row(b, j), 0)),
        ],
        out_shape=[jax.ShapeDtypeStruct((n, 2 * wd), BF16),
                   jax.ShapeDtypeStruct((n, 2 * wd), BF16),
                   jax.ShapeDtypeStruct((n, wd), BF16)],
        scratch_shapes=[pltpu.VMEM((1, LANES), F32)],
        compiler_params=_cparams(("arbitrary", "arbitrary")),
        name="fox_prep",
    )(p, p, p, p, _lane_vec(f_bias, 0), g2(q_norm_g), g2(k_norm_g))


FOX_TQ = 512


def _fox_flash_kernel(qa_ref, ka_ref, v_ref, z_ref, o_ref):
    i = pl.program_id(2)
    tq = FOX_TQ
    lane = _iota((tq, LANES), 1)
    lo = lane < FOX_HD
    diag = _iota((tq, tq), 0) >= _iota((tq, tq), 1)
    q = [qa_ref[:, hh * LANES:(hh + 1) * LANES] for hh in range(2)]

    def step(jb, carry, masked):
        start = pl.multiple_of(jb * tq, tq)
        vt = v_ref[pl.ds(start, tq), :]
        s = [lax.dot_general(q[hh], ka_ref[pl.ds(start, tq), hh * LANES:(hh + 1) * LANES],
                             (((1,), (1,)), ((), ())), preferred_element_type=F32)
             for hh in range(2)]
        if masked:
            s = [jnp.where(diag, x, -jnp.inf) for x in s]
        stats = []
        for hh in range(2):
            m_prev, l_prev, _ = carry[hh]
            m_new = jnp.maximum(m_prev, jnp.max(s[hh], axis=-1, keepdims=True))
            a = jnp.exp(m_prev - m_new)
            e = jnp.exp(s[hh] - m_new)
            stats.append((m_new, a * l_prev + jnp.sum(e, axis=-1, keepdims=True), a, e))
        return tuple((m_new, l_new, a * carry[hh][2]
                      + jnp.dot(e.astype(BF16), vt, preferred_element_type=F32))
                     for hh, (m_new, l_new, a, e) in enumerate(stats))

    init = tuple((jnp.full((tq, 1), -jnp.inf, F32), jnp.zeros((tq, 1), F32),
                  jnp.zeros((tq, LANES), F32)) for _ in range(2))
    carry = lax.fori_loop(0, i, lambda jb, cr: step(jb, cr, False), init)
    (_, l0, a0), (_, l1, a1) = step(i, carry, True)
    o = jnp.where(lo, a0 / l0, a1 / l1)
    o_ref[...] = (o * _silu(z_ref[...])).astype(o_ref.dtype)


def _fox_flash(qa, ka, vb, p, bsz, t_len):
    n = qa.shape[0]
    tq = FOX_TQ
    nq = t_len // tq
    wd = FOX_H * FOX_HD
    return pl.pallas_call(
        _fox_flash_kernel,
        grid=(bsz, FOX_H // 2, nq),
        in_specs=[
            pl.BlockSpec((tq, 2 * LANES), lambda b, hp, i: (b * nq + i, hp)),
            pl.BlockSpec((t_len, 2 * LANES), lambda b, hp, i: (b, hp)),
            pl.BlockSpec((t_len, LANES), lambda b, hp, i: (b, hp)),
            pl.BlockSpec((tq, LANES), lambda b, hp, i: (b * nq + i, 3 * wd // LANES + hp)),
        ],
        out_specs=pl.BlockSpec((tq, LANES), lambda b, hp, i: (b * nq + i, hp)),
        out_shape=jax.ShapeDtypeStruct((n, wd), BF16),
        compiler_params=_cparams(("parallel", "parallel", "arbitrary")),
        name="fox_flash",
    )(qa, ka, vb, p)


ML_TB = 256


def _mlstm_kernel(qk_ref, halo_ref, v_ref, og_ref, z_ref, gat_ref, cw_ref, gb_ref, ng_ref,
                  o_ref, c_ref, m_ref):
    j = pl.program_id(1)
    c, tb = CHUNK, ML_TB
    nch = tb // c
    probs = [(ci, h) for ci in range(nch) for h in range(ML_H)]

    @pl.when(j == 0)
    def _():
        c_ref[...] = jnp.zeros_like(c_ref)
        m_ref[...] = jnp.zeros_like(m_ref)

    halo = jnp.where(j > 0, halo_ref[...], 0.0)
    y = _conv_silu(qk_ref[...], halo, cw_ref[...])

    gt = gat_ref[...] + gb_ref[...]
    lane = _iota((tb, LANES), 1)
    rt, ct = _iota((tb, tb), 0), _iota((tb, tb), 1)
    tri_bd = ((rt >= ct) & (rt // c == ct // c)).astype(F32)
    bsum = _mm_f32(tri_bd, _log_sigmoid(gt))
    rows_t = _mm_nt_f32(_eye(LANES, LANES), jnp.where(lane < ML_H, gt, bsum))

    causal = _iota((c, c), 0) >= _iota((c, c), 1)
    lo = lane < ML_DQK
    ones_col = (lane == 0).astype(F32)
    ng = ng_ref[...]
    kbase = ML_H * ML_DQK

    def rows(a, ci):
        return a[ci * c:(ci + 1) * c]

    q, k, v_aug = {}, {}, {}
    e_d, w_inter, e_mt, kw, w_old = {}, {}, {}, {}, {}
    for h in range(ML_H):
        t, half = h // 2, h % 2
        own = lo if half == 0 else jnp.logical_not(lo)
        q[h] = jnp.where(own, y[:, t * LANES:(t + 1) * LANES], 0.0) * ML_DQK ** -0.5
        k[h] = jnp.where(own, y[:, kbase + t * LANES:kbase + (t + 1) * LANES], 0.0)
        v_aug[h] = jnp.concatenate([v_ref[:, h * ML_DV:(h + 1) * ML_DV], ones_col], axis=1)
        b_c = jnp.broadcast_to(bsum[:, ML_H + h:ML_H + h + 1], (tb, LANES))
        ig_c = jnp.broadcast_to(gt[:, h:h + 1], (tb, LANES))
        m_st = m_ref[h]
        for ci in range(nch):
            bc, igc = rows(b_c, ci), rows(ig_c, ci)
            b_r = rows_t[ML_H + h:ML_H + h + 1, ci * c:(ci + 1) * c]
            ig_r = rows_t[h:h + 1, ci * c:(ci + 1) * c]
            b_last = bc[c - 1:c, :]
            d = jnp.where(causal, bc[:, :c] - b_r + ig_r, -jnp.inf)
            m_intra = jnp.max(d, axis=-1, keepdims=True)
            a_end = b_last - bc + igc
            m_end = jnp.max(a_end, axis=0, keepdims=True)
            inter = bc + m_st
            m_t = jnp.maximum(inter, m_intra)
            m_new = jnp.maximum(b_last + m_st, m_end)
            e_d[ci, h] = jnp.exp(d - m_t[:, :c])
            w_inter[ci, h] = jnp.exp(inter - m_t)
            e_mt[ci, h] = jnp.exp(-m_t)
            w_old[ci, h] = jnp.exp(b_last + m_st - m_new)
            kw[ci, h] = rows(k[h], ci) * jnp.exp(a_end - m_new)
            m_st = m_new
        m_ref[h] = m_st

    sc = {(ci, h): _mm_nt(rows(q[h], ci), rows(k[h], ci)) for ci, h in probs}
    kv = {(ci, h): _mm_tn(kw[ci, h], rows(v_aug[h], ci)) for ci, h in probs}

    cst = [c_ref[h] for h in range(ML_H)]
    for ci in range(nch):
        qc = [_mm(rows(q[h], ci), cst[h]) for h in range(ML_H)]
        pv = [_mm(sc[ci, h] * e_d[ci, h], rows(v_aug[h], ci)) for h in range(ML_H)]
        for h in range(ML_H):
            wi, wo = w_inter[ci, h], w_old[ci, h]
            num = jnp.concatenate([wi, wi], axis=1) * qc[h] + pv[h]
            cst[h] = jnp.concatenate([wo, wo], axis=1) * cst[h] + kv[ci, h]
            den = num[:, ML_DV:ML_DV + 1]
            hv = num[:, :ML_DV] / jnp.maximum(jnp.abs(den), e_mt[ci, h])
            hv = hv * lax.rsqrt(jnp.mean(hv * hv, axis=-1, keepdims=True) + EPS) * ng
            rs, sl = slice(ci * c, (ci + 1) * c), slice(h * ML_DV, (h + 1) * ML_DV)
            o_ref[rs, sl] = (hv * _sigmoid(og_ref[rs, sl]) * _silu(z_ref[rs, sl])).astype(o_ref.dtype)
    for h in range(ML_H):
        c_ref[h] = cst[h]


def _mlstm_core(p, bsz, t_len, conv_w, gate_bias, norm_g):
    n = p.shape[0]
    tb = ML_TB
    nb = t_len // tb
    hb = tb // 8
    row = lambda b, j: b * nb + j
    return pl.pallas_call(
        _mlstm_kernel,
        grid=(bsz, nb),
        in_specs=[
            pl.BlockSpec((tb, 1024), lambda b, j: (row(b, j), 0)),
            pl.BlockSpec((8, 1024), lambda b, j: (jnp.maximum(row(b, j) * hb - 1, 0), 0)),
            pl.BlockSpec((tb, 1024), lambda b, j: (row(b, j), 1)),
            pl.BlockSpec((tb, 1024), lambda b, j: (row(b, j), 2)),
            pl.BlockSpec((tb, 1024), lambda b, j: (row(b, j), 3)),
            pl.BlockSpec((tb, LANES), lambda b, j: (row(b, j), 32)),
            pl.BlockSpec((CONV_K, 1024), lambda b, j: (0, 0)),
            pl.BlockSpec((1, LANES), lambda b, j: (0, 0)),
            pl.BlockSpec((1, LANES), lambda b, j: (0, 0)),
        ],
        out_specs=pl.BlockSpec((tb, 1024), lambda b, j: (row(b, j), 0)),
        out_shape=jax.ShapeDtypeStruct((n, 1024), BF16),
        scratch_shapes=[pltpu.VMEM((ML_H, LANES, 2 * ML_DV), F32),
                        pltpu.VMEM((ML_H, 1, LANES), F32)],
        compiler_params=_cparams(("arbitrary", "arbitrary")),
        name="mlstm_core",
    )(p, p, p, p, p, p, conv_w.astype(F32), _lane_vec(gate_bias, 0),
      norm_g.reshape(1, ML_DV).astype(F32))


def _gdn_layer(x2, bsz, t_len, w_in, conv_w, a_log, dt_bias, norm_g, w_out, ln_g, ln_b):
    p = _in_proj(x2, _pad_cols(w_in).astype(BF16))
    og = _gdn_core(p, bsz, t_len, conv_w, a_log, dt_bias, norm_g)
    return _out_ln(og, x2, w_out.astype(BF16), ln_g, ln_b)


def _swa_layer(x2, bsz, t_len, w_in, sinks, w_out, ln_g, ln_b):
    q_w, kv_w = SWA_H * SWA_HD, SWA_KVH * SWA_HD
    w_perm = jnp.concatenate([w_in[:, :q_w], w_in[:, q_w + 2 * kv_w:],
                              w_in[:, q_w:q_w + 2 * kv_w]], axis=1)
    p = _in_proj(x2, w_perm.astype(BF16))
    cos_t, sin_t = _rope_tables(t_len)
    og = _swa_core(p, bsz, t_len, sinks, cos_t, sin_t)
    return _out_ln(og, x2, w_out.astype(BF16), ln_g, ln_b)


def _fox_layer(x2, bsz, t_len, w_in, f_bias, q_norm_g, k_norm_g, w_out, ln_g, ln_b):
    p = _in_proj(x2, _pad_cols(w_in).astype(BF16))
    qa, ka, vb = _fox_prep(p, bsz, t_len, f_bias, q_norm_g, k_norm_g)
    og = _fox_flash(qa, ka, vb, p, bsz, t_len)
    return _out_ln(og, x2, w_out.astype(BF16), ln_g, ln_b)


def _mlstm_layer(x2, bsz, t_len, w_in, conv_w, gate_bias, norm_g, w_out, ln_g, ln_b):
    p = _in_proj(x2, _pad_cols(w_in).astype(BF16))
    og = _mlstm_core(p, bsz, t_len, conv_w, gate_bias, norm_g)
    return _out_ln(og, x2, w_out.astype(BF16), ln_g, ln_b)


def kernel(x, a_w_in, a_conv, a_a_log, a_dt_bias, a_norm_g, a_w_out, b_w_in, b_sinks, b_w_out, c_w_in, c_f_bias, c_q_norm, c_k_norm, c_w_out, d_w_in, d_conv, d_gate_bias, d_norm_g, d_w_out, ln_g, ln_b):
    bsz, t_len, d = x.shape
    x2 = x.reshape(bsz * t_len, d)
    for i in range(DEPTH):
        kind, j = i % 4, i // 4
        if kind == 0:
            x2 = _gdn_layer(x2, bsz, t_len, a_w_in[j], a_conv[j], a_a_log[j], a_dt_bias[j],
                            a_norm_g[j], a_w_out[j], ln_g[i], ln_b[i])
        elif kind == 1:
            x2 = _swa_layer(x2, bsz, t_len, b_w_in[j], b_sinks[j], b_w_out[j], ln_g[i], ln_b[i])
        elif kind == 2:
            x2 = _fox_layer(x2, bsz, t_len, c_w_in[j], c_f_bias[j], c_q_norm[j], c_k_norm[j],
                            c_w_out[j], ln_g[i], ln_b[i])
        else:
            x2 = _mlstm_layer(x2, bsz, t_len, d_w_in[j], d_conv[j], d_gate_bias[j], d_norm_g[j],
                              d_w_out[j], ln_g[i], ln_b[i])
    return x2.reshape(bsz, t_len, d)
```

```python
import functools
import math

import jax
import jax.numpy as jnp
from jax import lax
from jax.experimental import pallas as pl
from jax.experimental.pallas import tpu as pltpu

F32 = jnp.float32
BF16 = jnp.bfloat16
LANES = 128
VMEM_LIMIT = 56 * 1024 * 1024

D_MODEL = 1024
DEPTH = 4
CONV_K = 4
ROPE_THETA = 10000.0
EPS = 1e-6
LN_EPS = 1e-5
ALPHA = (2 * DEPTH) ** 0.25

GDN_H, GDN_DK, GDN_DV = 8, 128, 128
SWA_H, SWA_KVH, SWA_HD, SWA_W = 16, 2, 64, 128
FOX_H, FOX_HD = 16, 64
ML_H, ML_DQK, ML_DV = 8, 64, 128
CHUNK = 64


def _mm(a, b):
    return jnp.dot(a.astype(BF16), b.astype(BF16), preferred_element_type=F32)


def _mm_nt(a, b):
    return lax.dot_general(a.astype(BF16), b.astype(BF16), (((1,), (1,)), ((), ())),
                           preferred_element_type=F32)


def _mm_tn(a, b):
    return lax.dot_general(a.astype(BF16), b.astype(BF16), (((0,), (0,)), ((), ())),
                           preferred_element_type=F32)


def _mm_f32(a, b):
    return jnp.dot(a, b, preferred_element_type=F32, precision=lax.Precision.HIGHEST)


def _mm_nt_f32(a, b):
    return lax.dot_general(a, b, (((1,), (1,)), ((), ())), preferred_element_type=F32,
                           precision=lax.Precision.HIGHEST)


def _sigmoid(x):
    return 1.0 / (1.0 + jnp.exp(-x))


def _silu(x):
    return x * _sigmoid(x)


def _softplus(x):
    return jnp.maximum(x, 0.0) + jnp.log1p(jnp.exp(-jnp.abs(x)))


def _log_sigmoid(x):
    return -_softplus(-x)


def _iota(shape, dim):
    return lax.broadcasted_iota(jnp.int32, shape, dim)


def _tri(n):
    return (_iota((n, n), 0) >= _iota((n, n), 1)).astype(F32)


def _eye(n, m):
    return (_iota((n, m), 0) == _iota((n, m), 1)).astype(F32)


def _conv_silu(x, halo, w):
    n = x.shape[0]
    xx = jnp.concatenate([halo, x], axis=0)
    y = w[CONV_K - 1:CONV_K] * x
    for j in range(1, CONV_K):
        y = y + w[CONV_K - 1 - j:CONV_K - j] * pltpu.roll(xx, j, 0)[8:8 + n]
    return _silu(y)


def _cparams(sem):
    return pltpu.CompilerParams(dimension_semantics=sem, vmem_limit_bytes=VMEM_LIMIT)


def _pad_cols(w, mult=LANES):
    pad = (-w.shape[1]) % mult
    return jnp.pad(w, ((0, 0), (0, pad))) if pad else w


def _lane_vec(vals, offset):
    return jnp.zeros((1, LANES), F32).at[0, offset:offset + vals.shape[0]].set(vals.astype(F32))


def _const_spec(shape):
    return pl.BlockSpec(shape, lambda *_: (0,) * len(shape), pipeline_mode=pl.Buffered(1))


def _residual_ln(xb, og, w_out, g, b):
    y = ALPHA * xb + _mm(og, w_out)
    mu = jnp.mean(y, axis=-1, keepdims=True)
    yc = y - mu
    var = jnp.mean(yc * yc, axis=-1, keepdims=True)
    return yc * lax.rsqrt(var + LN_EPS) * g + b


def _out_ln_kernel(og_ref, x_ref, w_ref, g_ref, b_ref, o_ref):
    o_ref[...] = _residual_ln(x_ref[...], og_ref[...], w_ref[...], g_ref[...], b_ref[...])


def _out_ln(og, x2, w, g, b):
    n, d = x2.shape
    wd = og.shape[1]
    tm = 512
    return pl.pallas_call(
        _out_ln_kernel,
        grid=(n // tm,),
        in_specs=[pl.BlockSpec((tm, wd), lambda i: (i, 0)),
                  pl.BlockSpec((tm, d), lambda i: (i, 0)),
                  pl.BlockSpec((wd, d), lambda i: (0, 0)),
                  pl.BlockSpec((1, d), lambda i: (0, 0)),
                  pl.BlockSpec((1, d), lambda i: (0, 0))],
        out_specs=pl.BlockSpec((tm, d), lambda i: (i, 0)),
        out_shape=jax.ShapeDtypeStruct((n, d), F32),
        compiler_params=_cparams(("parallel",)),
        name="out_proj_ln",
    )(og, x2, w, g.reshape(1, d).astype(F32), b.reshape(1, d).astype(F32))


GDN_TB = 256


def _gdn_kernel(x_ref, win_ref, cw_ref, alog_ref, dtb_ref, ng_ref, wout_ref, lg_ref, lb_ref,
                o_ref, s_ref, halo_ref, og_ref):
    j = pl.program_id(1)
    c, tb = CHUNK, GDN_TB
    nch = tb // c
    probs = [(ci, h) for ci in range(nch) for h in range(GDN_H)]
    cw = 3 * GDN_H * 128

    @pl.when(j == 0)
    def _():
        s_ref[...] = jnp.zeros_like(s_ref)
        halo_ref[...] = jnp.zeros_like(halo_ref)

    xb = x_ref[...]
    p = _mm(xb, win_ref[...])
    qkv = p[:, :cw]
    y = _conv_silu(qkv, halo_ref[...], cw_ref[...])
    halo_ref[...] = qkv[tb - 8:tb]
    z = p[:, cw:cw + GDN_H * GDN_DV]

    gat = p[:, cw + GDN_H * GDN_DV:cw + GDN_H * GDN_DV + LANES]
    beta = _sigmoid(gat)
    g = -jnp.exp(alog_ref[...]) * _softplus(gat + dtb_ref[...])
    rt, ct = _iota((tb, tb), 0), _iota((tb, tb), 1)
    tri_bd = ((rt >= ct) & (rt // c == ct // c)).astype(F32)
    gc = _mm_f32(tri_bd, g)
    gct = _mm_nt_f32(_eye(LANES, LANES), gc)

    row = _iota((c, c), 0)
    col = _iota((c, c), 1)
    causal = row >= col
    strict = row > col
    ng = ng_ref[...]
    qoff, koff, voff = 0, GDN_H * GDN_DK, 2 * GDN_H * GDN_DK

    kn, kb, qn, qd, kd, rhs, gl = {}, {}, {}, {}, {}, {}, {}
    for h in range(GDN_H):
        q = y[:, qoff + h * 128:qoff + (h + 1) * 128]
        k = y[:, koff + h * 128:koff + (h + 1) * 128]
        v = y[:, voff + h * 128:voff + (h + 1) * 128]
        q = q * (lax.rsqrt(jnp.sum(q * q, axis=-1, keepdims=True) + EPS) * GDN_DK ** -0.5)
        k = k * lax.rsqrt(jnp.sum(k * k, axis=-1, keepdims=True) + EPS)
        beta_c = beta[:, h:h + 1]
        gc_c = jnp.broadcast_to(gc[:, 8 + h:9 + h], (tb, LANES))
        egc = jnp.exp(gc_c)
        g_last = jnp.concatenate(
            [jnp.broadcast_to(gc_c[(ci + 1) * c - 1:(ci + 1) * c, :], (c, LANES))
             for ci in range(nch)], axis=0)
        kbh = k * beta_c
        kn[h], kb[h], qn[h] = k, kbh, q
        qd[h] = q * egc
        kd[h] = k * jnp.exp(g_last - gc_c)
        rhs[h] = jnp.concatenate([v * beta_c, kbh * egc], axis=1)
        gl[h] = g_last

    def rows(a, ci):
        return a[ci * c:(ci + 1) * c]

    neg_l, qk = {}, {}
    for ci, h in probs:
        gmat = _mm_nt(jnp.concatenate([rows(kb[h], ci), rows(qn[h], ci)], axis=0), rows(kn[h], ci))
        gc_c = jnp.broadcast_to(gc[ci * c:(ci + 1) * c, 8 + h:9 + h], (c, c))
        gc_r = gct[8 + h:9 + h, ci * c:(ci + 1) * c]
        decay = jnp.exp(jnp.where(causal, gc_c - gc_r, -jnp.inf))
        neg_l[ci, h] = jnp.where(strict, -(gmat[:c] * decay), 0.0)
        qk[ci, h] = jnp.where(causal, gmat[c:] * decay, 0.0)
    pw = dict(neg_l)
    m = {key: _mm(neg_l[key], neg_l[key]) for key in probs}
    nsq = int(math.log2(c)) - 1
    for it in range(nsq):
        if it < nsq - 1:
            r = {key: _mm(jnp.concatenate([pw[key], m[key]], axis=0), m[key]) for key in probs}
            pw = {key: pw[key] + m[key] + r[key][:c] for key in probs}
            m = {key: r[key][c:] for key in probs}
        else:
            pw = {key: pw[key] + m[key] + _mm(pw[key], m[key]) for key in probs}
    uw = {}
    for ci, h in probs:
        rh = rows(rhs[h], ci)
        uw[ci, h] = rh + _mm(pw[ci, h], rh)

    s = [s_ref[h] for h in range(GDN_H)]
    for ci in range(nch):
        t = [_mm(jnp.concatenate([uw[ci, h][:, 128:], rows(qd[h], ci)], axis=0), s[h])
             for h in range(GDN_H)]
        v_new = [uw[ci, h][:, :128] - t[h][:c] for h in range(GDN_H)]
        o = [t[h][c:] + _mm(qk[ci, h], v_new[h]) for h in range(GDN_H)]
        s = [s[h] * jnp.exp(gl[h][ci * c:ci * c + 1, :]) + _mm_tn(rows(kd[h], ci), v_new[h])
             for h in range(GDN_H)]
        for h in range(GDN_H):
            oh = o[h] * lax.rsqrt(jnp.mean(o[h] * o[h], axis=-1, keepdims=True) + EPS) * ng
            zt = z[ci * c:(ci + 1) * c, h * 128:(h + 1) * 128]
            og_ref[ci * c:(ci + 1) * c, h * 128:(h + 1) * 128] = (oh * _silu(zt)).astype(og_ref.dtype)
    for h in range(GDN_H):
        s_ref[h] = s[h]
    o_ref[...] = _residual_ln(xb, og_ref[...], wout_ref[...], lg_ref[...], lb_ref[...])


def _gdn_layer(x2, bsz, t_len, w_in, conv_w, a_log, dt_bias, norm_g, w_out, ln_g, ln_b):
    n, d = x2.shape
    tb = GDN_TB
    nb = t_len // tb
    cw = 3 * GDN_H * 128
    w_in = _pad_cols(w_in).astype(BF16)
    row = lambda b, j: (b * nb + j, 0)
    return pl.pallas_call(
        _gdn_kernel,
        grid=(bsz, nb),
        in_specs=[
            pl.BlockSpec((tb, d), row),
            _const_spec(w_in.shape),
            _const_spec((CONV_K, cw)),
            _const_spec((1, LANES)),
            _const_spec((1, LANES)),
            _const_spec((1, LANES)),
            _const_spec(w_out.shape),
            _const_spec((1, d)),
            _const_spec((1, d)),
        ],
        out_specs=pl.BlockSpec((tb, d), row),
        out_shape=jax.ShapeDtypeStruct((n, d), F32),
        scratch_shapes=[pltpu.VMEM((GDN_H, GDN_DK, GDN_DV), F32),
                        pltpu.VMEM((8, cw), F32),
                        pltpu.VMEM((tb, GDN_H * GDN_DV), BF16)],
        compiler_params=_cparams(("arbitrary", "arbitrary")),
        name="gdn_layer",
    )(x2, w_in, conv_w.astype(F32), _lane_vec(a_log, 8), _lane_vec(dt_bias, 8),
      norm_g.reshape(1, GDN_DV).astype(F32), w_out.astype(BF16),
      ln_g.reshape(1, d).astype(F32), ln_b.reshape(1, d).astype(F32))


def _rope_tile(x, cos, sin_signed, first_half):
    rot = jnp.where(first_half, pltpu.roll(x, LANES - SWA_HD // 2, 1), pltpu.roll(x, SWA_HD // 2, 1))
    return x * cos + rot * sin_signed


SWA_TB = 256


def _swa_kernel(sink_ref, x_ref, win_ref, cos_ref, sin_ref, wout_ref, lg_ref, lb_ref,
                o_ref, kp_ref, vp_ref, og_ref):
    j = pl.program_id(1)
    w, tb = SWA_W, SWA_TB
    nw = tb // w
    q_w, kv_w = SWA_H * SWA_HD, SWA_KVH * SWA_HD

    @pl.when(j == 0)
    def _():
        kp_ref[...] = jnp.zeros_like(kp_ref)
        vp_ref[...] = jnp.zeros_like(vp_ref)

    xb = x_ref[...]
    p = _mm(xb, win_ref[...])
    z = p[:, q_w + 2 * kv_w:]
    cos = cos_ref[...]
    sin = sin_ref[...]
    lane = _iota((w, LANES), 1)
    lo = lane < SWA_HD
    lane_t = _iota((tb, LANES), 1)
    first_half = (lane_t % SWA_HD) < SWA_HD // 2
    lo2 = _iota((2 * w, LANES), 1) < SWA_HD

    kr = _rope_tile(p[:, q_w:q_w + kv_w], cos, sin, first_half)
    vc = p[:, q_w + kv_w:q_w + 2 * kv_w]
    kall = jnp.concatenate([kp_ref[...], kr], axis=0)
    vall = jnp.concatenate([vp_ref[...], vc], axis=0)
    kp_ref[...] = kr[tb - w:]
    vp_ref[...] = vc[tb - w:]

    r = _iota((2 * w, 2 * w), 0) % w
    cidx = _iota((2 * w, 2 * w), 1)
    in_prev = (cidx < w) & (cidx > r)
    in_cur = (cidx >= w) & (cidx - w <= r)
    top = _iota((2 * w, 1), 0) < w

    group = SWA_H // SWA_KVH
    tiles = [(wi, t) for wi in range(nw) for t in range(SWA_H // 2)]
    kdup, vdup = {}, {}
    for wi in range(nw):
        kcat = kall[wi * w:(wi + 2) * w]
        vcat = vall[wi * w:(wi + 2) * w]
        kswap = pltpu.roll(kcat, SWA_HD, 1)
        vswap = pltpu.roll(vcat, SWA_HD, 1)
        for kvh in range(SWA_KVH):
            own = lo2 if kvh == 0 else jnp.logical_not(lo2)
            kdup[wi, kvh] = jnp.where(own, kcat, kswap)
            vdup[wi, kvh] = jnp.where(own, vcat, vswap)
    qr = {}
    for t in range(SWA_H // 2):
        qr[t] = _rope_tile(p[:, t * LANES:(t + 1) * LANES], cos, sin, first_half) * SWA_HD ** -0.5
    scores = {}
    for wi, t in tiles:
        qt = qr[t][wi * w:(wi + 1) * w]
        q2 = jnp.concatenate([jnp.where(lo, qt, 0.0), jnp.where(lo, 0.0, qt)], axis=0)
        valid = ((in_prev & (j > 0)) | in_cur) if wi == 0 else (in_prev | in_cur)
        scores[wi, t] = jnp.where(valid, _mm_nt(q2, kdup[wi, t // (group // 2)]), -jnp.inf)
    probs = {}
    for wi, t in tiles:
        s = scores[wi, t]
        sk = jnp.where(top, sink_ref[2 * t], sink_ref[2 * t + 1])
        m = jnp.maximum(jnp.max(s, axis=-1, keepdims=True), sk)
        e = jnp.exp(s - m)
        den = jnp.sum(e, axis=-1, keepdims=True) + jnp.exp(sk - m)
        probs[wi, t] = e / den
    for wi, t in tiles:
        o2 = _mm(probs[wi, t], vdup[wi, t // (group // 2)])
        ot = jnp.where(lo, o2[:w], o2[w:])
        zt = z[wi * w:(wi + 1) * w, t * LANES:(t + 1) * LANES]
        og_ref[wi * w:(wi + 1) * w, t * LANES:(t + 1) * LANES] = (ot * _silu(zt)).astype(og_ref.dtype)
    o_ref[...] = _residual_ln(xb, og_ref[...], wout_ref[...], lg_ref[...], lb_ref[...])


def _swa_layer(x2, bsz, t_len, w_in, sinks, w_out, ln_g, ln_b):
    n, d = x2.shape
    tb = SWA_TB
    nb = t_len // tb
    q_w = SWA_H * SWA_HD
    cos_t, sin_t = _rope_tables(t_len)
    row = lambda b, j: (b * nb + j, 0)
    return pl.pallas_call(
        _swa_kernel,
        grid=(bsz, nb),
        in_specs=[
            pl.BlockSpec(memory_space=pltpu.SMEM),
            pl.BlockSpec((tb, d), row),
            _const_spec(w_in.shape),
            pl.BlockSpec((tb, LANES), lambda b, j: (j, 0)),
            pl.BlockSpec((tb, LANES), lambda b, j: (j, 0)),
            _const_spec(w_out.shape),
            _const_spec((1, d)),
            _const_spec((1, d)),
        ],
        out_specs=pl.BlockSpec((tb, d), row),
        out_shape=jax.ShapeDtypeStruct((n, d), F32),
        scratch_shapes=[pltpu.VMEM((SWA_W, LANES), F32), pltpu.VMEM((SWA_W, LANES), F32),
                        pltpu.VMEM((tb, q_w), BF16)],
        compiler_params=_cparams(("arbitrary", "arbitrary")),
        name="swa_layer",
    )(sinks.astype(F32), x2, w_in.astype(BF16), cos_t, sin_t, w_out.astype(BF16),
      ln_g.reshape(1, d).astype(F32), ln_b.reshape(1, d).astype(F32))


def _rope_tables(t_len):
    half = SWA_HD // 2
    inv = ROPE_THETA ** (-jnp.arange(half, dtype=F32) / half)
    ang = jnp.arange(t_len, dtype=F32)[:, None] * inv[None, :]
    cos, sin = jnp.cos(ang), jnp.sin(ang)
    return (jnp.concatenate([cos, cos, cos, cos], axis=1),
            jnp.concatenate([-sin, sin, -sin, sin], axis=1))


FOX_BIAS_LANES = 3


def _split3(x):
    x1 = x.astype(BF16).astype(F32)
    r1 = x - x1
    x2 = r1.astype(BF16).astype(F32)
    x3 = (r1 - x2).astype(BF16).astype(F32)
    return x1, x2, x3


FOX_TB = 256


def _fox_prep_kernel(x_ref, win_ref, fb_ref, qg_ref, kg_ref,
                     qa_ref, ka_ref, vb_ref, z_ref, carry_ref):
    j = pl.program_id(1)
    tb = FOX_TB
    wd = FOX_H * FOX_HD

    @pl.when(j == 0)
    def _():
        carry_ref[...] = jnp.zeros_like(carry_ref)

    p = _mm(x_ref[...], win_ref[...])
    z_ref[...] = p[:, 3 * wd:4 * wd]
    logf = _log_sigmoid(p[:, 4 * wd:4 * wd + LANES] + fb_ref[...])
    cum = _mm_f32(_tri(tb), logf) + carry_ref[...]
    carry_ref[...] = cum[tb - 1:tb, :]

    lane = _iota((tb, LANES), 1)
    ones2 = ((_iota((LANES, LANES), 0) // FOX_HD) == (_iota((LANES, LANES), 1) // FOX_HD)).astype(BF16)

    def norm(x, g):
        x2 = x * x
        hi = x2.astype(BF16)
        lo = (x2 - hi.astype(F32)).astype(BF16)
        ss = (jnp.dot(hi, ones2, preferred_element_type=F32)
              + jnp.dot(lo, ones2, preferred_element_type=F32))
        return x * lax.rsqrt(ss * (1.0 / FOX_HD) + EPS) * g

    head_lane = lane < FOX_HD
    bl = lane - FOX_HD
    for t in range(FOX_H // 2):
        sl = slice(t * LANES, (t + 1) * LANES)
        qn = norm(p[:, t * LANES:(t + 1) * LANES], qg_ref[...]) * FOX_HD ** -0.5
        kn = norm(p[:, wd + t * LANES:wd + (t + 1) * LANES], kg_ref[...])
        vb_ref[:, sl] = p[:, 2 * wd + t * LANES:2 * wd + (t + 1) * LANES].astype(BF16)
        for half in range(2):
            h = 2 * t + half
            c1, c2, c3 = _split3(jnp.broadcast_to(cum[:, h:h + 1], (tb, LANES)))
            qb = jnp.where(bl == 0, c1, jnp.where(bl == 1, c2, jnp.where(bl == 2, c3,
                 jnp.where((bl >= 3) & (bl < 6), 1.0, 0.0))))
            kb = jnp.where(bl == 3, -c1, jnp.where(bl == 4, -c2, jnp.where(bl == 5, -c3,
                 jnp.where((bl >= 0) & (bl < 3), 1.0, 0.0))))
            qh = qn if half == 0 else pltpu.roll(qn, FOX_HD, 1)
            kh = kn if half == 0 else pltpu.roll(kn, FOX_HD, 1)
            qa_ref[:, h * LANES:(h + 1) * LANES] = jnp.where(head_lane, qh, qb).astype(BF16)
            ka_ref[:, h * LANES:(h + 1) * LANES] = jnp.where(head_lane, kh, kb).astype(BF16)


def _fox_prep(x2, bsz, t_len, w_in, f_bias, q_norm_g, k_norm_g):
    n, d = x2.shape
    tb = FOX_TB
    nb = t_len // tb
    wd = FOX_H * FOX_HD
    w_in = _pad_cols(w_in).astype(BF16)
    row = lambda b, j: (b * nb + j, 0)
    g2 = lambda g: jnp.concatenate([g, g]).reshape(1, LANES).astype(F32)
    return pl.pallas_call(
        _fox_prep_kernel,
        grid=(bsz, nb),
        in_specs=[
            pl.BlockSpec((tb, d), row),
            _const_spec(w_in.shape),
            _const_spec((1, LANES)),
            _const_spec((1, LANES)),
            _const_spec((1, LANES)),
        ],
        out_specs=[
            pl.BlockSpec((tb, 2 * wd), row),
            pl.BlockSpec((tb, 2 * wd), row),
            pl.BlockSpec((tb, wd), row),
            pl.BlockSpec((tb, wd), row),
        ],
        out_shape=[jax.ShapeDtypeStruct((n, 2 * wd), BF16),
                   jax.ShapeDtypeStruct((n, 2 * wd), BF16),
                   jax.ShapeDtypeStruct((n, wd), BF16),
                   jax.ShapeDtypeStruct((n, wd), F32)],
        scratch_shapes=[pltpu.VMEM((1, LANES), F32)],
        compiler_params=_cparams(("arbitrary", "arbitrary")),
        name="fox_prep",
    )(x2, w_in, _lane_vec(f_bias, 0), g2(q_norm_g), g2(k_norm_g))


FOX_TQ = 512


def _fox_flash_kernel(qa_ref, ka_ref, v_ref, z_ref, o_ref):
    i = pl.program_id(2)
    tq = FOX_TQ
    lane = _iota((tq, LANES), 1)
    lo = lane < FOX_HD
    diag = _iota((tq, tq), 0) >= _iota((tq, tq), 1)
    q = [qa_ref[:, hh * LANES:(hh + 1) * LANES] for hh in range(2)]

    def step(jb, carry, masked):
        start = pl.multiple_of(jb * tq, tq)
        vt = v_ref[pl.ds(start, tq), :]
        s = [lax.dot_general(q[hh], ka_ref[pl.ds(start, tq), hh * LANES:(hh + 1) * LANES],
                             (((1,), (1,)), ((), ())), preferred_element_type=F32)
             for hh in range(2)]
        if masked:
            s = [jnp.where(diag, x, -jnp.inf) for x in s]
        stats = []
        for hh in range(2):
            m_prev, l_prev, _ = carry[hh]
            m_new = jnp.maximum(m_prev, jnp.max(s[hh], axis=-1, keepdims=True))
            a = jnp.exp(m_prev - m_new)
            e = jnp.exp(s[hh] - m_new)
            stats.append((m_new, a * l_prev + jnp.sum(e, axis=-1, keepdims=True), a, e))
        return tuple((m_new, l_new, a * carry[hh][2]
                      + jnp.dot(e.astype(BF16), vt, preferred_element_type=F32))
                     for hh, (m_new, l_new, a, e) in enumerate(stats))

    init = tuple((jnp.full((tq, 1), -jnp.inf, F32), jnp.zeros((tq, 1), F32),
                  jnp.zeros((tq, LANES), F32)) for _ in range(2))
    carry = lax.fori_loop(0, i, lambda jb, cr: step(jb, cr, False), init)
    (_, l0, a0), (_, l1, a1) = step(i, carry, True)
    o = jnp.where(lo, a0 / l0, a1 / l1)
    o_ref[...] = (o * _silu(z_ref[...])).astype(o_ref.dtype)


def _fox_flash(qa, ka, vb, z, bsz, t_len):
    n = qa.shape[0]
    tq = FOX_TQ
    nq = t_len // tq
    wd = FOX_H * FOX_HD
    return pl.pallas_call(
        _fox_flash_kernel,
        grid=(bsz, FOX_H // 2, nq),
        in_specs=[
            pl.BlockSpec((tq, 2 * LANES), lambda b, hp, i: (b * nq + i, hp)),
            pl.BlockSpec((t_len, 2 * LANES), lambda b, hp, i: (b, hp)),
            pl.BlockSpec((t_len, LANES), lambda b, hp, i: (b, hp)),
            pl.BlockSpec((tq, LANES), lambda b, hp, i: (b * nq + i, hp)),
        ],
        out_specs=pl.BlockSpec((tq, LANES), lambda b, hp, i: (b * nq + i, hp)),
        out_shape=jax.ShapeDtypeStruct((n, wd), BF16),
        compiler_params=_cparams(("parallel", "parallel", "arbitrary")),
        name="fox_flash",
    )(qa, ka, vb, z)


ML_TB = 256


def _mlstm_kernel(x_ref, win_ref, cw_ref, gb_ref, ng_ref, wout_ref, lg_ref, lb_ref,
                  o_ref, c_ref, m_ref, halo_ref, og_ref):
    j = pl.program_id(1)
    c, tb = CHUNK, ML_TB
    nch = tb // c
    probs = [(ci, h) for ci in range(nch) for h in range(ML_H)]
    qkw, vw = 2 * ML_H * ML_DQK, ML_H * ML_DV

    @pl.when(j == 0)
    def _():
        c_ref[...] = jnp.zeros_like(c_ref)
        m_ref[...] = jnp.zeros_like(m_ref)
        halo_ref[...] = jnp.zeros_like(halo_ref)

    xb = x_ref[...]
    p = _mm(xb, win_ref[...])
    qk_raw = p[:, :qkw]
    y = _conv_silu(qk_raw, halo_ref[...], cw_ref[...])
    halo_ref[...] = qk_raw[tb - 8:tb]
    v_all = p[:, qkw:qkw + vw]
    o_gate = p[:, qkw + vw:qkw + 2 * vw]
    z = p[:, qkw + 2 * vw:qkw + 3 * vw]

    gt = p[:, qkw + 3 * vw:qkw + 3 * vw + LANES] + gb_ref[...]
    lane = _iota((tb, LANES), 1)
    rt, ct = _iota((tb, tb), 0), _iota((tb, tb), 1)
    tri_bd = ((rt >= ct) & (rt // c == ct // c)).astype(F32)
    bsum = _mm_f32(tri_bd, _log_sigmoid(gt))
    rows_t = _mm_nt_f32(_eye(LANES, LANES), jnp.where(lane < ML_H, gt, bsum))

    causal = _iota((c, c), 0) >= _iota((c, c), 1)
    lo = lane < ML_DQK
    ones_col = (lane == 0).astype(F32)
    ng = ng_ref[...]
    kbase = ML_H * ML_DQK

    def rows(a, ci):
        return a[ci * c:(ci + 1) * c]

    q, k, v_aug = {}, {}, {}
    e_d, w_inter, e_mt, kw, w_old = {}, {}, {}, {}, {}
    for h in range(ML_H):
        t, half = h // 2, h % 2
        own = lo if half == 0 else jnp.logical_not(lo)
        q[h] = jnp.where(own, y[:, t * LANES:(t + 1) * LANES], 0.0) * ML_DQK ** -0.5
        k[h] = jnp.where(own, y[:, kbase + t * LANES:kbase + (t + 1) * LANES], 0.0)
        v_aug[h] = jnp.concatenate([v_all[:, h * ML_DV:(h + 1) * ML_DV], ones_col], axis=1)
        b_c = jnp.broadcast_to(bsum[:, ML_H + h:ML_H + h + 1], (tb, LANES))
        ig_c = jnp.broadcast_to(gt[:, h:h + 1], (tb, LANES))
        m_st = m_ref[h]
        for ci in range(nch):
            bc, igc = rows(b_c, ci), rows(ig_c, ci)
            b_r = rows_t[ML_H + h:ML_H + h + 1, ci * c:(ci + 1) * c]
            ig_r = rows_t[h:h + 1, ci * c:(ci + 1) * c]
            b_last = bc[c - 1:c, :]
            d = jnp.where(causal, bc[:, :c] - b_r + ig_r, -jnp.inf)
            m_intra = jnp.max(d, axis=-1, keepdims=True)
            a_end = b_last - bc + igc
            m_end = jnp.max(a_end, axis=0, keepdims=True)
            inter = bc + m_st
            m_t = jnp.maximum(inter, m_intra)
            m_new = jnp.maximum(b_last + m_st, m_end)
            e_d[ci, h] = jnp.exp(d - m_t[:, :c])
            w_inter[ci, h] = jnp.exp(inter - m_t)
            e_mt[ci, h] = jnp.exp(-m_t)
            w_old[ci, h] = jnp.exp(b_last + m_st - m_new)
            kw[ci, h] = rows(k[h], ci) * jnp.exp(a_end - m_new)
            m_st = m_new
        m_ref[h] = m_st

    sc = {(ci, h): _mm_nt(rows(q[h], ci), rows(k[h], ci)) for ci, h in probs}
    kv = {(ci, h): _mm_tn(kw[ci, h], rows(v_aug[h], ci)) for ci, h in probs}

    cst = [c_ref[h] for h in range(ML_H)]
    for ci in range(nch):
        qc = [_mm(rows(q[h], ci), cst[h]) for h in range(ML_H)]
        pv = [_mm(sc[ci, h] * e_d[ci, h], rows(v_aug[h], ci)) for h in range(ML_H)]
        for h in range(ML_H):
            wi, wo = w_inter[ci, h], w_old[ci, h]
            num = jnp.concatenate([wi, wi], axis=1) * qc[h] + pv[h]
            cst[h] = jnp.concatenate([wo, wo], axis=1) * cst[h] + kv[ci, h]
            den = num[:, ML_DV:ML_DV + 1]
            hv = num[:, :ML_DV] / jnp.maximum(jnp.abs(den), e_mt[ci, h])
            hv = hv * lax.rsqrt(jnp.mean(hv * hv, axis=-1, keepdims=True) + EPS) * ng
            rs, sl = slice(ci * c, (ci + 1) * c), slice(h * ML_DV, (h + 1) * ML_DV)
            og_ref[rs, sl] = (hv * _sigmoid(o_gate[rs, sl]) * _silu(z[rs, sl])).astype(og_ref.dtype)
    for h in range(ML_H):
        c_ref[h] = cst[h]
    o_ref[...] = _residual_ln(xb, og_ref[...], wout_ref[...], lg_ref[...], lb_ref[...])


def _mlstm_layer(x2, bsz, t_len, w_in, conv_w, gate_bias, norm_g, w_out, ln_g, ln_b):
    n, d = x2.shape
    tb = ML_TB
    nb = t_len // tb
    qkw = 2 * ML_H * ML_DQK
    w_in = _pad_cols(w_in).astype(BF16)
    row = lambda b, j: (b * nb + j, 0)
    return pl.pallas_call(
        _mlstm_kernel,
        grid=(bsz, nb),
        in_specs=[
            pl.BlockSpec((tb, d), row),
            _const_spec(w_in.shape),
            _const_spec((CONV_K, qkw)),
            _const_spec((1, LANES)),
            _const_spec((1, LANES)),
            _const_spec(w_out.shape),
            _const_spec((1, d)),
            _const_spec((1, d)),
        ],
        out_specs=pl.BlockSpec((tb, d), row),
        out_shape=jax.ShapeDtypeStruct((n, d), F32),
        scratch_shapes=[pltpu.VMEM((ML_H, LANES, 2 * ML_DV), F32),
                        pltpu.VMEM((ML_H, 1, LANES), F32),
                        pltpu.VMEM((8, qkw), F32),
                        pltpu.VMEM((tb, ML_H * ML_DV), BF16)],
        compiler_params=_cparams(("arbitrary", "arbitrary")),
        name="mlstm_layer",
    )(x2, w_in, conv_w.astype(F32), _lane_vec(gate_bias, 0), norm_g.reshape(1, ML_DV).astype(F32),
      w_out.astype(BF16), ln_g.reshape(1, d).astype(F32), ln_b.reshape(1, d).astype(F32))


def _fox_layer(x2, bsz, t_len, w_in, f_bias, q_norm_g, k_norm_g, w_out, ln_g, ln_b):
    qa, ka, vb, z = _fox_prep(x2, bsz, t_len, w_in, f_bias, q_norm_g, k_norm_g)
    og = _fox_flash(qa, ka, vb, z, bsz, t_len)
    return _out_ln(og, x2, w_out.astype(BF16), ln_g, ln_b)


def kernel(x, a_w_in, a_conv, a_a_log, a_dt_bias, a_norm_g, a_w_out, b_w_in, b_sinks, b_w_out, c_w_in, c_f_bias, c_q_norm, c_k_norm, c_w_out, d_w_in, d_conv, d_gate_bias, d_norm_g, d_w_out, ln_g, ln_b):
    bsz, t_len, d = x.shape
    x2 = x.reshape(bsz * t_len, d)
    for i in range(DEPTH):
        kind, j = i % 4, i // 4
        if kind == 0:
            x2 = _gdn_layer(x2, bsz, t_len, a_w_in[j], a_conv[j], a_a_log[j], a_dt_bias[j],
                            a_norm_g[j], a_w_out[j], ln_g[i], ln_b[i])
        elif kind == 1:
            x2 = _swa_layer(x2, bsz, t_len, b_w_in[j], b_sinks[j], b_w_out[j], ln_g[i], ln_b[i])
        elif kind == 2:
            x2 = _fox_layer(x2, bsz, t_len, c_w_in[j], c_f_bias[j], c_q_norm[j], c_k_norm[j],
                            c_w_out[j], ln_g[i], ln_b[i])
        else:
            x2 = _mlstm_layer(x2, bsz, t_len, d_w_in[j], d_conv[j], d_gate_bias[j], d_norm_g[j],
                              d_w_out[j], ln_g[i], ln_b[i])
    return x2.reshape(bsz, t_len, d)
```

```python
import functools
import math

import jax
import jax.numpy as jnp
from jax import lax
from jax.experimental import pallas as pl
from jax.experimental.pallas import tpu as pltpu

F32 = jnp.float32
BF16 = jnp.bfloat16
LANES = 128
VMEM_LIMIT = 56 * 1024 * 1024

D_MODEL = 1024
DEPTH = 4
CONV_K = 4
ROPE_THETA = 10000.0
EPS = 1e-6
LN_EPS = 1e-5
ALPHA = (2 * DEPTH) ** 0.25
LOG2E = math.log2(math.e)

GDN_H, GDN_DK, GDN_DV = 8, 128, 128
SWA_H, SWA_KVH, SWA_HD, SWA_W = 16, 2, 64, 128
FOX_H, FOX_HD = 16, 64
ML_H, ML_DQK, ML_DV = 8, 64, 128
CHUNK = 64


def _mm(a, b):
    return jnp.dot(a.astype(BF16), b.astype(BF16), preferred_element_type=F32)


def _mm_nt(a, b):
    return lax.dot_general(a.astype(BF16), b.astype(BF16), (((1,), (1,)), ((), ())),
                           preferred_element_type=F32)


def _mm_tn(a, b):
    return lax.dot_general(a.astype(BF16), b.astype(BF16), (((0,), (0,)), ((), ())),
                           preferred_element_type=F32)


def _mm_f32(a, b):
    return jnp.dot(a, b, preferred_element_type=F32, precision=lax.Precision.HIGHEST)


def _mm_nt_f32(a, b):
    return lax.dot_general(a, b, (((1,), (1,)), ((), ())), preferred_element_type=F32,
                           precision=lax.Precision.HIGHEST)


def _sigmoid(x):
    return 1.0 / (1.0 + jnp.exp(-x))


def _silu(x):
    return x * _sigmoid(x)


def _softplus(x):
    return jnp.maximum(x, 0.0) + jnp.log1p(jnp.exp(-jnp.abs(x)))


def _log_sigmoid(x):
    return -_softplus(-x)


def _iota(shape, dim):
    return lax.broadcasted_iota(jnp.int32, shape, dim)


def _tri(n):
    return (_iota((n, n), 0) >= _iota((n, n), 1)).astype(F32)


def _eye(n, m):
    return (_iota((n, m), 0) == _iota((n, m), 1)).astype(F32)


def _conv_silu(x, halo, w):
    n = x.shape[0]
    xx = jnp.concatenate([halo, x], axis=0)
    y = w[CONV_K - 1:CONV_K] * x
    for j in range(1, CONV_K):
        y = y + w[CONV_K - 1 - j:CONV_K - j] * pltpu.roll(xx, j, 0)[8:8 + n]
    return _silu(y)


def _cparams(sem):
    return pltpu.CompilerParams(dimension_semantics=sem, vmem_limit_bytes=VMEM_LIMIT)


def _pad_cols(w, mult=LANES):
    pad = (-w.shape[1]) % mult
    return jnp.pad(w, ((0, 0), (0, pad))) if pad else w


def _lane_vec(vals, offset):
    return jnp.zeros((1, LANES), F32).at[0, offset:offset + vals.shape[0]].set(vals.astype(F32))


def _const_spec(shape):
    return pl.BlockSpec(shape, lambda *_: (0,) * len(shape), pipeline_mode=pl.Buffered(1))


def _residual_ln(xb, og, w_out, g, b):
    y = ALPHA * xb + _mm(og, w_out)
    mu = jnp.mean(y, axis=-1, keepdims=True)
    yc = y - mu
    var = jnp.mean(yc * yc, axis=-1, keepdims=True)
    return yc * lax.rsqrt(var + LN_EPS) * g + b


def _out_ln_kernel(og_ref, x_ref, w_ref, g_ref, b_ref, o_ref):
    o_ref[...] = _residual_ln(x_ref[...], og_ref[...], w_ref[...], g_ref[...], b_ref[...])


def _out_ln(og, x2, w, g, b):
    n, d = x2.shape
    wd = og.shape[1]
    tm = 512
    return pl.pallas_call(
        _out_ln_kernel,
        grid=(n // tm,),
        in_specs=[pl.BlockSpec((tm, wd), lambda i: (i, 0)),
                  pl.BlockSpec((tm, d), lambda i: (i, 0)),
                  pl.BlockSpec((wd, d), lambda i: (0, 0)),
                  pl.BlockSpec((1, d), lambda i: (0, 0)),
                  pl.BlockSpec((1, d), lambda i: (0, 0))],
        out_specs=pl.BlockSpec((tm, d), lambda i: (i, 0)),
        out_shape=jax.ShapeDtypeStruct((n, d), F32),
        compiler_params=_cparams(("parallel",)),
        name="out_proj_ln",
    )(og, x2, w, g.reshape(1, d).astype(F32), b.reshape(1, d).astype(F32))


GDN_TB = 256


def _gdn_kernel(x_ref, win_ref, cw_ref, alog_ref, dtb_ref, ng_ref, wout_ref, lg_ref, lb_ref,
                o_ref, s_ref, halo_ref, og_ref):
    j = pl.program_id(1)
    c, tb = CHUNK, GDN_TB
    nch = tb // c
    probs = [(ci, h) for ci in range(nch) for h in range(GDN_H)]
    cw = 3 * GDN_H * 128

    @pl.when(j == 0)
    def _():
        s_ref[...] = jnp.zeros_like(s_ref)
        halo_ref[...] = jnp.zeros_like(halo_ref)

    xb = x_ref[...]
    p = _mm(xb, win_ref[...])
    qkv = p[:, :cw]
    y = _conv_silu(qkv, halo_ref[...], cw_ref[...])
    halo_ref[...] = qkv[tb - 8:tb]
    z = p[:, cw:cw + GDN_H * GDN_DV]

    gat = p[:, cw + GDN_H * GDN_DV:cw + GDN_H * GDN_DV + LANES]
    beta = _sigmoid(gat)
    g = -jnp.exp(alog_ref[...]) * _softplus(gat + dtb_ref[...])
    rt, ct = _iota((tb, tb), 0), _iota((tb, tb), 1)
    tri_bd = ((rt >= ct) & (rt // c == ct // c)).astype(F32)
    gc = _mm_f32(tri_bd, g)
    gct = _mm_nt_f32(_eye(LANES, LANES), gc)

    row = _iota((c, c), 0)
    col = _iota((c, c), 1)
    causal = row >= col
    strict = row > col
    ng = ng_ref[...]
    qoff, koff, voff = 0, GDN_H * GDN_DK, 2 * GDN_H * GDN_DK

    kn, kb, qn, qd, kd, rhs, gl = {}, {}, {}, {}, {}, {}, {}
    for h in range(GDN_H):
        q = y[:, qoff + h * 128:qoff + (h + 1) * 128]
        k = y[:, koff + h * 128:koff + (h + 1) * 128]
        v = y[:, voff + h * 128:voff + (h + 1) * 128]
        q = q * (lax.rsqrt(jnp.sum(q * q, axis=-1, keepdims=True) + EPS) * GDN_DK ** -0.5)
        k = k * lax.rsqrt(jnp.sum(k * k, axis=-1, keepdims=True) + EPS)
        beta_c = beta[:, h:h + 1]
        gc_c = jnp.broadcast_to(gc[:, 8 + h:9 + h], (tb, LANES))
        egc = jnp.exp(gc_c)
        g_last = jnp.concatenate(
            [jnp.broadcast_to(gc_c[(ci + 1) * c - 1:(ci + 1) * c, :], (c, LANES))
             for ci in range(nch)], axis=0)
        kbh = k * beta_c
        kn[h], kb[h], qn[h] = k, kbh, q
        qd[h] = q * egc
        kd[h] = k * jnp.exp(g_last - gc_c)
        rhs[h] = jnp.concatenate([v * beta_c, kbh * egc], axis=1)
        gl[h] = g_last

    def rows(a, ci):
        return a[ci * c:(ci + 1) * c]

    neg_l, qk = {}, {}
    for ci, h in probs:
        gmat = _mm_nt(jnp.concatenate([rows(kb[h], ci), rows(qn[h], ci)], axis=0), rows(kn[h], ci))
        gc_c = jnp.broadcast_to(gc[ci * c:(ci + 1) * c, 8 + h:9 + h], (c, c))
        gc_r = gct[8 + h:9 + h, ci * c:(ci + 1) * c]
        decay = jnp.exp(jnp.where(causal, gc_c - gc_r, -jnp.inf))
        neg_l[ci, h] = jnp.where(strict, -(gmat[:c] * decay), 0.0)
        qk[ci, h] = jnp.where(causal, gmat[c:] * decay, 0.0)
    pw = dict(neg_l)
    m = {key: _mm(neg_l[key], neg_l[key]) for key in probs}
    nsq = int(math.log2(c)) - 1
    for it in range(nsq):
        if it < nsq - 1:
            r = {key: _mm(jnp.concatenate([pw[key], m[key]], axis=0), m[key]) for key in probs}
            pw = {key: pw[key] + m[key] + r[key][:c] for key in probs}
            m = {key: r[key][c:] for key in probs}
        else:
            pw = {key: pw[key] + m[key] + _mm(pw[key], m[key]) for key in probs}
    uw = {}
    for ci, h in probs:
        rh = rows(rhs[h], ci)
        uw[ci, h] = rh + _mm(pw[ci, h], rh)

    s = [s_ref[h] for h in range(GDN_H)]
    for ci in range(nch):
        t = [_mm(jnp.concatenate([uw[ci, h][:, 128:], rows(qd[h], ci)], axis=0), s[h])
             for h in range(GDN_H)]
        v_new = [uw[ci, h][:, :128] - t[h][:c] for h in range(GDN_H)]
        o = [t[h][c:] + _mm(qk[ci, h], v_new[h]) for h in range(GDN_H)]
        s = [s[h] * jnp.exp(gl[h][ci * c:ci * c + 1, :]) + _mm_tn(rows(kd[h], ci), v_new[h])
             for h in range(GDN_H)]
        for h in range(GDN_H):
            oh = o[h] * lax.rsqrt(jnp.mean(o[h] * o[h], axis=-1, keepdims=True) + EPS) * ng
            zt = z[ci * c:(ci + 1) * c, h * 128:(h + 1) * 128]
            og_ref[ci * c:(ci + 1) * c, h * 128:(h + 1) * 128] = (oh * _silu(zt)).astype(og_ref.dtype)
    for h in range(GDN_H):
        s_ref[h] = s[h]
    o_ref[...] = _residual_ln(xb, og_ref[...], wout_ref[...], lg_ref[...], lb_ref[...])


def _gdn_layer(x2, bsz, t_len, w_in, conv_w, a_log, dt_bias, norm_g, w_out, ln_g, ln_b):
    n, d = x2.shape
    tb = GDN_TB
    nb = t_len // tb
    cw = 3 * GDN_H * 128
    w_in = _pad_cols(w_in).astype(BF16)
    row = lambda b, j: (b * nb + j, 0)
    return pl.pallas_call(
        _gdn_kernel,
        grid=(bsz, nb),
        in_specs=[
            pl.BlockSpec((tb, d), row),
            _const_spec(w_in.shape),
            _const_spec((CONV_K, cw)),
            _const_spec((1, LANES)),
            _const_spec((1, LANES)),
            _const_spec((1, LANES)),
            _const_spec(w_out.shape),
            _const_spec((1, d)),
            _const_spec((1, d)),
        ],
        out_specs=pl.BlockSpec((tb, d), row),
        out_shape=jax.ShapeDtypeStruct((n, d), F32),
        scratch_shapes=[pltpu.VMEM((GDN_H, GDN_DK, GDN_DV), F32),
                        pltpu.VMEM((8, cw), F32),
                        pltpu.VMEM((tb, GDN_H * GDN_DV), BF16)],
        compiler_params=_cparams(("arbitrary", "arbitrary")),
        name="gdn_layer",
    )(x2, w_in, conv_w.astype(F32), _lane_vec(a_log, 8), _lane_vec(dt_bias, 8),
      norm_g.reshape(1, GDN_DV).astype(F32), w_out.astype(BF16),
      ln_g.reshape(1, d).astype(F32), ln_b.reshape(1, d).astype(F32))


def _rope_tile(x, cos, sin_signed, first_half):
    rot = jnp.where(first_half, pltpu.roll(x, LANES - SWA_HD // 2, 1), pltpu.roll(x, SWA_HD // 2, 1))
    return x * cos + rot * sin_signed


SWA_TB = 256


def _swa_kernel(sink_ref, x_ref, win_ref, cos_ref, sin_ref, wout_ref, lg_ref, lb_ref,
                o_ref, kp_ref, vp_ref, og_ref):
    j = pl.program_id(1)
    w, tb = SWA_W, SWA_TB
    nw = tb // w
    q_w, kv_w = SWA_H * SWA_HD, SWA_KVH * SWA_HD

    @pl.when(j == 0)
    def _():
        kp_ref[...] = jnp.zeros_like(kp_ref)
        vp_ref[...] = jnp.zeros_like(vp_ref)

    xb = x_ref[...]
    p = _mm(xb, win_ref[...])
    z = p[:, q_w + 2 * kv_w:]
    cos = cos_ref[...]
    sin = sin_ref[...]
    lane = _iota((w, LANES), 1)
    lo = lane < SWA_HD
    lane_t = _iota((tb, LANES), 1)
    first_half = (lane_t % SWA_HD) < SWA_HD // 2
    lo2 = _iota((2 * w, LANES), 1) < SWA_HD

    kr = _rope_tile(p[:, q_w:q_w + kv_w], cos, sin, first_half)
    vc = p[:, q_w + kv_w:q_w + 2 * kv_w]
    kall = jnp.concatenate([kp_ref[...], kr], axis=0)
    vall = jnp.concatenate([vp_ref[...], vc], axis=0)
    kp_ref[...] = kr[tb - w:]
    vp_ref[...] = vc[tb - w:]

    r = _iota((2 * w, 2 * w), 0) % w
    cidx = _iota((2 * w, 2 * w), 1)
    in_prev = (cidx < w) & (cidx > r)
    in_cur = (cidx >= w) & (cidx - w <= r)
    top = _iota((2 * w, 1), 0) < w

    group = SWA_H // SWA_KVH
    tiles = [(wi, t) for wi in range(nw) for t in range(SWA_H // 2)]
    kdup, vdup = {}, {}
    for wi in range(nw):
        kcat = kall[wi * w:(wi + 2) * w]
        vcat = vall[wi * w:(wi + 2) * w]
        kswap = pltpu.roll(kcat, SWA_HD, 1)
        vswap = pltpu.roll(vcat, SWA_HD, 1)
        for kvh in range(SWA_KVH):
            own = lo2 if kvh == 0 else jnp.logical_not(lo2)
            kdup[wi, kvh] = jnp.where(own, kcat, kswap)
            vdup[wi, kvh] = jnp.where(own, vcat, vswap)
    qr = {}
    for t in range(SWA_H // 2):
        qr[t] = _rope_tile(p[:, t * LANES:(t + 1) * LANES], cos, sin, first_half) * SWA_HD ** -0.5
    scores = {}
    for wi, t in tiles:
        qt = qr[t][wi * w:(wi + 1) * w]
        q2 = jnp.concatenate([jnp.where(lo, qt, 0.0), jnp.where(lo, 0.0, qt)], axis=0)
        valid = ((in_prev & (j > 0)) | in_cur) if wi == 0 else (in_prev | in_cur)
        scores[wi, t] = jnp.where(valid, _mm_nt(q2, kdup[wi, t // (group // 2)]), -jnp.inf)
    probs = {}
    for wi, t in tiles:
        s = scores[wi, t]
        sk = jnp.where(top, sink_ref[2 * t], sink_ref[2 * t + 1])
        m = jnp.maximum(jnp.max(s, axis=-1, keepdims=True), sk)
        e = jnp.exp(s - m)
        den = jnp.sum(e, axis=-1, keepdims=True) + jnp.exp(sk - m)
        probs[wi, t] = e / den
    for wi, t in tiles:
        o2 = _mm(probs[wi, t], vdup[wi, t // (group // 2)])
        ot = jnp.where(lo, o2[:w], o2[w:])
        zt = z[wi * w:(wi + 1) * w, t * LANES:(t + 1) * LANES]
        og_ref[wi * w:(wi + 1) * w, t * LANES:(t + 1) * LANES] = (ot * _silu(zt)).astype(og_ref.dtype)
    o_ref[...] = _residual_ln(xb, og_ref[...], wout_ref[...], lg_ref[...], lb_ref[...])


def _swa_layer(x2, bsz, t_len, w_in, sinks, w_out, ln_g, ln_b):
    n, d = x2.shape
    tb = SWA_TB
    nb = t_len // tb
    q_w = SWA_H * SWA_HD
    cos_t, sin_t = _rope_tables(t_len)
    row = lambda b, j: (b * nb + j, 0)
    return pl.pallas_call(
        _swa_kernel,
        grid=(bsz, nb),
        in_specs=[
            pl.BlockSpec(memory_space=pltpu.SMEM),
            pl.BlockSpec((tb, d), row),
            _const_spec(w_in.shape),
            pl.BlockSpec((tb, LANES), lambda b, j: (j, 0)),
            pl.BlockSpec((tb, LANES), lambda b, j: (j, 0)),
            _const_spec(w_out.shape),
            _const_spec((1, d)),
            _const_spec((1, d)),
        ],
        out_specs=pl.BlockSpec((tb, d), row),
        out_shape=jax.ShapeDtypeStruct((n, d), F32),
        scratch_shapes=[pltpu.VMEM((SWA_W, LANES), F32), pltpu.VMEM((SWA_W, LANES), F32),
                        pltpu.VMEM((tb, q_w), BF16)],
        compiler_params=_cparams(("arbitrary", "arbitrary")),
        name="swa_layer",
    )(sinks.astype(F32), x2, w_in.astype(BF16), cos_t, sin_t, w_out.astype(BF16),
      ln_g.reshape(1, d).astype(F32), ln_b.reshape(1, d).astype(F32))


def _rope_tables(t_len):
    half = SWA_HD // 2
    inv = ROPE_THETA ** (-jnp.arange(half, dtype=F32) / half)
    ang = jnp.arange(t_len, dtype=F32)[:, None] * inv[None, :]
    cos, sin = jnp.cos(ang), jnp.sin(ang)
    return (jnp.concatenate([cos, cos, cos, cos], axis=1),
            jnp.concatenate([-sin, sin, -sin, sin], axis=1))


FOX_BIAS_LANES = 3


def _split3(x):
    x1 = x.astype(BF16).astype(F32)
    r1 = x - x1
    x2 = r1.astype(BF16).astype(F32)
    x3 = (r1 - x2).astype(BF16).astype(F32)
    return x1, x2, x3


FOX_TB = 256
FOX_VR = FOX_HD + 16


def _fox_prep_kernel(x_ref, win_ref, fb_ref, qg_ref, kg_ref,
                     qa_ref, ka_ref, vt_ref, z_ref, carry_ref):
    j = pl.program_id(1)
    tb = FOX_TB
    wd = FOX_H * FOX_HD

    @pl.when(j == 0)
    def _():
        carry_ref[...] = jnp.zeros_like(carry_ref)

    p = _mm(x_ref[...], win_ref[...])
    z_ref[...] = p[:, 3 * wd:4 * wd]
    logf = _log_sigmoid(p[:, 4 * wd:4 * wd + LANES] + fb_ref[...])
    cum = _mm_f32(_tri(tb), logf) + carry_ref[...]
    carry_ref[...] = cum[tb - 1:tb, :]
    cum2 = cum * LOG2E

    lane = _iota((tb, LANES), 1)
    ones2 = ((_iota((LANES, LANES), 0) // FOX_HD) == (_iota((LANES, LANES), 1) // FOX_HD)).astype(BF16)

    def norm(x, g):
        x2 = x * x
        hi = x2.astype(BF16)
        lo = (x2 - hi.astype(F32)).astype(BF16)
        ss = (jnp.dot(hi, ones2, preferred_element_type=F32)
              + jnp.dot(lo, ones2, preferred_element_type=F32))
        return x * lax.rsqrt(ss * (1.0 / FOX_HD) + EPS) * g

    head_lane = lane < FOX_HD
    bl = lane - FOX_HD
    ones_rows = (_iota((FOX_VR - FOX_HD, tb), 0) == 0).astype(BF16)
    for t in range(FOX_H // 2):
        sl = slice(t * LANES, (t + 1) * LANES)
        qn = norm(p[:, t * LANES:(t + 1) * LANES], qg_ref[...]) * (FOX_HD ** -0.5 * LOG2E)
        kn = norm(p[:, wd + t * LANES:wd + (t + 1) * LANES], kg_ref[...])
        vt2 = p[:, 2 * wd + t * LANES:2 * wd + (t + 1) * LANES].T.astype(BF16)
        for half in range(2):
            h = 2 * t + half
            c1, c2, c3 = _split3(jnp.broadcast_to(cum2[:, h:h + 1], (tb, LANES)))
            qb = jnp.where(bl == 0, c1, jnp.where(bl == 1, c2, jnp.where(bl == 2, c3,
                 jnp.where((bl >= 3) & (bl < 6), 1.0, 0.0))))
            kb = jnp.where(bl == 3, -c1, jnp.where(bl == 4, -c2, jnp.where(bl == 5, -c3,
                 jnp.where((bl >= 0) & (bl < 3), 1.0, 0.0))))
            qh = qn if half == 0 else pltpu.roll(qn, FOX_HD, 1)
            kh = kn if half == 0 else pltpu.roll(kn, FOX_HD, 1)
            qa_ref[:, h * LANES:(h + 1) * LANES] = jnp.where(head_lane, qh, qb).astype(BF16)
            ka_ref[:, h * LANES:(h + 1) * LANES] = jnp.where(head_lane, kh, kb).astype(BF16)
            vt_ref[h * FOX_VR:h * FOX_VR + FOX_HD, :] = vt2[half * FOX_HD:(half + 1) * FOX_HD]
            vt_ref[h * FOX_VR + FOX_HD:(h + 1) * FOX_VR, :] = ones_rows


def _fox_prep(x2, bsz, t_len, w_in, f_bias, q_norm_g, k_norm_g):
    n, d = x2.shape
    tb = FOX_TB
    nb = t_len // tb
    wd = FOX_H * FOX_HD
    w_in = _pad_cols(w_in).astype(BF16)
    row = lambda b, j: (b * nb + j, 0)
    g2 = lambda g: jnp.concatenate([g, g]).reshape(1, LANES).astype(F32)
    return pl.pallas_call(
        _fox_prep_kernel,
        grid=(bsz, nb),
        in_specs=[
            pl.BlockSpec((tb, d), row),
            _const_spec(w_in.shape),
            _const_spec((1, LANES)),
            _const_spec((1, LANES)),
            _const_spec((1, LANES)),
        ],
        out_specs=[
            pl.BlockSpec((tb, 2 * wd), row),
            pl.BlockSpec((tb, 2 * wd), row),
            pl.BlockSpec((FOX_H * FOX_VR, tb), lambda b, j: (b, j)),
            pl.BlockSpec((tb, wd), row),
        ],
        out_shape=[jax.ShapeDtypeStruct((n, 2 * wd), BF16),
                   jax.ShapeDtypeStruct((n, 2 * wd), BF16),
                   jax.ShapeDtypeStruct((bsz * FOX_H * FOX_VR, t_len), BF16),
                   jax.ShapeDtypeStruct((n, wd), F32)],
        scratch_shapes=[pltpu.VMEM((1, LANES), F32)],
        compiler_params=_cparams(("arbitrary", "arbitrary")),
        name="fox_prep",
    )(x2, w_in, _lane_vec(f_bias, 0), g2(q_norm_g), g2(k_norm_g))


FOX_TQ = 512
FOX_HPS = 4
FOX_KC = 16


def _fox_flash_kernel(qa_ref, ka_ref, vt_ref, z_ref, o_ref, s_scr, e_scr, m_scr, acc_scr):
    i = pl.program_id(2)
    tq, nh = FOX_TQ, FOX_HPS
    m_scr[...] = jnp.full(m_scr.shape, -jnp.inf, F32)
    acc_scr[...] = jnp.zeros_like(acc_scr)
    q = [qa_ref[:, hh * LANES:(hh + 1) * LANES] for hh in range(nh)]
    causal_c = _iota((FOX_KC, tq), 0) - _iota((FOX_KC, tq), 1)

    def block(jb, masked):
        start = pl.multiple_of(jb * tq, tq)
        for hh in range(nh):
            s_scr[hh] = lax.dot_general(
                ka_ref[pl.ds(start, tq), hh * LANES:(hh + 1) * LANES], q[hh],
                (((1,), (1,)), ((), ())), preferred_element_type=F32)

        def scores(hh, c):
            s = s_scr[hh, c * FOX_KC:(c + 1) * FOX_KC, :]
            return jnp.where(causal_c <= -c * FOX_KC, s, -jnp.inf) if masked else s

        nkc = tq // FOX_KC
        for hh in range(nh):
            m_prev = m_scr[hh]
            mx = functools.reduce(jnp.maximum, [scores(hh, c) for c in range(nkc)])
            m_new = jnp.maximum(m_prev, jnp.max(mx, axis=0, keepdims=True))
            a = jnp.exp2(m_prev - m_new)
            for c in range(nkc):
                e_scr[hh, c * FOX_KC:(c + 1) * FOX_KC, :] = jnp.exp2(scores(hh, c) - m_new).astype(BF16)
            m_scr[hh] = m_new
            vt = vt_ref[hh * FOX_VR:(hh + 1) * FOX_VR, pl.ds(start, tq)]
            acc_scr[hh] = a * acc_scr[hh] + jnp.dot(vt, e_scr[hh], preferred_element_type=F32)

    def body(jb, carry):
        block(jb, False)
        return carry

    lax.fori_loop(0, i, body, 0)
    block(i, True)
    for t in range(nh // 2):
        ot = jnp.concatenate([acc_scr[hh, :FOX_HD] / acc_scr[hh, FOX_HD:FOX_HD + 1]
                              for hh in (2 * t, 2 * t + 1)], axis=0)
        sl = slice(t * LANES, (t + 1) * LANES)
        o_ref[:, sl] = (ot.T * _silu(z_ref[:, sl])).astype(o_ref.dtype)


def _fox_flash(qa, ka, vt, z, bsz, t_len):
    n = qa.shape[0]
    tq = FOX_TQ
    nq = t_len // tq
    wd = FOX_H * FOX_HD
    nh = FOX_HPS
    ng = FOX_H // nh
    return pl.pallas_call(
        _fox_flash_kernel,
        grid=(bsz, ng, nq),
        in_specs=[
            pl.BlockSpec((tq, nh * LANES), lambda b, hg, i: (b * nq + i, hg)),
            pl.BlockSpec((t_len, nh * LANES), lambda b, hg, i: (b, hg)),
            pl.BlockSpec((nh * FOX_VR, t_len), lambda b, hg, i: (b * ng + hg, 0)),
            pl.BlockSpec((tq, nh // 2 * LANES), lambda b, hg, i: (b * nq + i, hg)),
        ],
        out_specs=pl.BlockSpec((tq, nh // 2 * LANES), lambda b, hg, i: (b * nq + i, hg)),
        out_shape=jax.ShapeDtypeStruct((n, wd), BF16),
        scratch_shapes=[pltpu.VMEM((nh, tq, tq), F32), pltpu.VMEM((nh, tq, tq), BF16),
                        pltpu.VMEM((nh, 1, tq), F32), pltpu.VMEM((nh, FOX_VR, tq), F32)],
        compiler_params=_cparams(("parallel", "parallel", "arbitrary")),
        name="fox_flash",
    )(qa, ka, vt, z)


ML_TB = 256


def _mlstm_kernel(x_ref, win_ref, cw_ref, gb_ref, ng_ref, wout_ref, lg_ref, lb_ref,
                  o_ref, c_ref, m_ref, halo_ref, og_ref):
    j = pl.program_id(1)
    c, tb = CHUNK, ML_TB
    nch = tb // c
    probs = [(ci, h) for ci in range(nch) for h in range(ML_H)]
    qkw, vw = 2 * ML_H * ML_DQK, ML_H * ML_DV

    @pl.when(j == 0)
    def _():
        c_ref[...] = jnp.zeros_like(c_ref)
        m_ref[...] = jnp.zeros_like(m_ref)
        halo_ref[...] = jnp.zeros_like(halo_ref)

    xb = x_ref[...]
    p = _mm(xb, win_ref[...])
    qk_raw = p[:, :qkw]
    y = _conv_silu(qk_raw, halo_ref[...], cw_ref[...])
    halo_ref[...] = qk_raw[tb - 8:tb]
    v_all = p[:, qkw:qkw + vw]
    o_gate = p[:, qkw + vw:qkw + 2 * vw]
    z = p[:, qkw + 2 * vw:qkw + 3 * vw]

    gt = p[:, qkw + 3 * vw:qkw + 3 * vw + LANES] + gb_ref[...]
    lane = _iota((tb, LANES), 1)
    rt, ct = _iota((tb, tb), 0), _iota((tb, tb), 1)
    tri_bd = ((rt >= ct) & (rt // c == ct // c)).astype(F32)
    bsum = _mm_f32(tri_bd, _log_sigmoid(gt))
    rows_t = _mm_nt_f32(_eye(LANES, LANES), jnp.where(lane < ML_H, gt, bsum))

    causal = _iota((c, c), 0) >= _iota((c, c), 1)
    lo = lane < ML_DQK
    ones_col = (lane == 0).astype(F32)
    ng = ng_ref[...]
    kbase = ML_H * ML_DQK

    def rows(a, ci):
        return a[ci * c:(ci + 1) * c]

    q, k, v_aug = {}, {}, {}
    e_d, w_inter, e_mt, kw, w_old = {}, {}, {}, {}, {}
    for h in range(ML_H):
        t, half = h // 2, h % 2
        own = lo if half == 0 else jnp.logical_not(lo)
        q[h] = jnp.where(own, y[:, t * LANES:(t + 1) * LANES], 0.0) * ML_DQK ** -0.5
        k[h] = jnp.where(own, y[:, kbase + t * LANES:kbase + (t + 1) * LANES], 0.0)
        v_aug[h] = jnp.concatenate([v_all[:, h * ML_DV:(h + 1) * ML_DV], ones_col], axis=1)
        b_c = jnp.broadcast_to(bsum[:, ML_H + h:ML_H + h + 1], (tb, LANES))
        ig_c = jnp.broadcast_to(gt[:, h:h + 1], (tb, LANES))
        m_st = m_ref[h]
        for ci in range(nch):
            bc, igc = rows(b_c, ci), rows(ig_c, ci)
            b_r = rows_t[ML_H + h:ML_H + h + 1, ci * c:(ci + 1) * c]
            ig_r = rows_t[h:h + 1, ci * c:(ci + 1) * c]
            b_last = bc[c - 1:c, :]
            d = jnp.where(causal, bc[:, :c] - b_r + ig_r, -jnp.inf)
            m_intra = jnp.max(d, axis=-1, keepdims=True)
            a_end = b_last - bc + igc
            m_end = jnp.max(a_end, axis=0, keepdims=True)
            inter = bc + m_st
            m_t = jnp.maximum(inter, m_intra)
            m_new = jnp.maximum(b_last + m_st, m_end)
            e_d[ci, h] = jnp.exp(d - m_t[:, :c])
            w_inter[ci, h] = jnp.exp(inter - m_t)
            e_mt[ci, h] = jnp.exp(-m_t)
            w_old[ci, h] = jnp.exp(b_last + m_st - m_new)
            kw[ci, h] = rows(k[h], ci) * jnp.exp(a_end - m_new)
            m_st = m_new
        m_ref[h] = m_st

    sc = {(ci, h): _mm_nt(rows(q[h], ci), rows(k[h], ci)) for ci, h in probs}
    kv = {(ci, h): _mm_tn(kw[ci, h], rows(v_aug[h], ci)) for ci, h in probs}

    cst = [c_ref[h] for h in range(ML_H)]
    for ci in range(nch):
        qc = [_mm(rows(q[h], ci), cst[h]) for h in range(ML_H)]
        pv = [_mm(sc[ci, h] * e_d[ci, h], rows(v_aug[h], ci)) for h in range(ML_H)]
        for h in range(ML_H):
            wi, wo = w_inter[ci, h], w_old[ci, h]
            num = jnp.concatenate([wi, wi], axis=1) * qc[h] + pv[h]
            cst[h] = jnp.concatenate([wo, wo], axis=1) * cst[h] + kv[ci, h]
            den = num[:, ML_DV:ML_DV + 1]
            hv = num[:, :ML_DV] / jnp.maximum(jnp.abs(den), e_mt[ci, h])
            hv = hv * lax.rsqrt(jnp.mean(hv * hv, axis=-1, keepdims=True) + EPS) * ng
            rs, sl = slice(ci * c, (ci + 1) * c), slice(h * ML_DV, (h + 1) * ML_DV)
            og_ref[rs, sl] = (hv * _sigmoid(o_gate[rs, sl]) * _silu(z[rs, sl])).astype(og_ref.dtype)
    for h in range(ML_H):
        c_ref[h] = cst[h]
    o_ref[...] = _residual_ln(xb, og_ref[...], wout_ref[...], lg_ref[...], lb_ref[...])


def _mlstm_layer(x2, bsz, t_len, w_in, conv_w, gate_bias, norm_g, w_out, ln_g, ln_b):
    n, d = x2.shape
    tb = ML_TB
    nb = t_len // tb
    qkw = 2 * ML_H * ML_DQK
    w_in = _pad_cols(w_in).astype(BF16)
    row = lambda b, j: (b * nb + j, 0)
    return pl.pallas_call(
        _mlstm_kernel,
        grid=(bsz, nb),
        in_specs=[
            pl.BlockSpec((tb, d), row),
            _const_spec(w_in.shape),
            _const_spec((CONV_K, qkw)),
            _const_spec((1, LANES)),
            _const_spec((1, LANES)),
            _const_spec(w_out.shape),
            _const_spec((1, d)),
            _const_spec((1, d)),
        ],
        out_specs=pl.BlockSpec((tb, d), row),
        out_shape=jax.ShapeDtypeStruct((n, d), F32),
        scratch_shapes=[pltpu.VMEM((ML_H, LANES, 2 * ML_DV), F32),
                        pltpu.VMEM((ML_H, 1, LANES), F32),
                        pltpu.VMEM((8, qkw), F32),
                        pltpu.VMEM((tb, ML_H * ML_DV), BF16)],
        compiler_params=_cparams(("arbitrary", "arbitrary")),
        name="mlstm_layer",
    )(x2, w_in, conv_w.astype(F32), _lane_vec(gate_bias, 0), norm_g.reshape(1, ML_DV).astype(F32),
      w_out.astype(BF16), ln_g.reshape(1, d).astype(F32), ln_b.reshape(1, d).astype(F32))


def _fox_layer(x2, bsz, t_len, w_in, f_bias, q_norm_g, k_norm_g, w_out, ln_g, ln_b):
    qa, ka, vt, z = _fox_prep(x2, bsz, t_len, w_in, f_bias, q_norm_g, k_norm_g)
    og = _fox_flash(qa, ka, vt, z, bsz, t_len)
    return _out_ln(og, x2, w_out.astype(BF16), ln_g, ln_b)


def kernel(x, a_w_in, a_conv, a_a_log, a_dt_bias, a_norm_g, a_w_out, b_w_in, b_sinks, b_w_out, c_w_in, c_f_bias, c_q_norm, c_k_norm, c_w_out, d_w_in, d_conv, d_gate_bias, d_norm_g, d_w_out, ln_g, ln_b):
    bsz, t_len, d = x.shape
    x2 = x.reshape(bsz * t_len, d)
    for i in range(DEPTH):
        kind, j = i % 4, i // 4
        if kind == 0:
            x2 = _gdn_layer(x2, bsz, t_len, a_w_in[j], a_conv[j], a_a_log[j], a_dt_bias[j],
                            a_norm_g[j], a_w_out[j], ln_g[i], ln_b[i])
        elif kind == 1:
            x2 = _swa_layer(x2, bsz, t_len, b_w_in[j], b_sinks[j], b_w_out[j], ln_g[i], ln_b[i])
        elif kind == 2:
            x2 = _fox_layer(x2, bsz, t_len, c_w_in[j], c_f_bias[j], c_q_norm[j], c_k_norm[j],
                            c_w_out[j], ln_g[i], ln_b[i])
        else:
            x2 = _mlstm_layer(x2, bsz, t_len, d_w_in[j], d_conv[j], d_gate_bias[j], d_norm_g[j],
                              d_w_out[j], ln_g[i], ln_b[i])
    return x2.reshape(bsz, t_len, d)
```

```python
import functools
import math

import jax
import jax.numpy as jnp
from jax import lax
from jax.experimental import pallas as pl
from jax.experimental.pallas import tpu as pltpu

F32 = jnp.float32
BF16 = jnp.bfloat16
LANES = 128
VMEM_LIMIT = 56 * 1024 * 1024

D_MODEL = 1024
DEPTH = 4
CONV_K = 4
ROPE_THETA = 10000.0
EPS = 1e-6
LN_EPS = 1e-5
ALPHA = (2 * DEPTH) ** 0.25
LOG2E = math.log2(math.e)

GDN_H, GDN_DK, GDN_DV = 8, 128, 128
SWA_H, SWA_KVH, SWA_HD, SWA_W = 16, 2, 64, 128
FOX_H, FOX_HD = 16, 64
ML_H, ML_DQK, ML_DV = 8, 64, 128
CHUNK = 64


def _mm(a, b):
    return jnp.dot(a.astype(BF16), b.astype(BF16), preferred_element_type=F32)


def _mm_nt(a, b):
    return lax.dot_general(a.astype(BF16), b.astype(BF16), (((1,), (1,)), ((), ())),
                           preferred_element_type=F32)


def _mm_tn(a, b):
    return lax.dot_general(a.astype(BF16), b.astype(BF16), (((0,), (0,)), ((), ())),
                           preferred_element_type=F32)


def _mm_f32(a, b):
    return jnp.dot(a, b, preferred_element_type=F32, precision=lax.Precision.HIGHEST)


def _mm_nt_f32(a, b):
    return lax.dot_general(a, b, (((1,), (1,)), ((), ())), preferred_element_type=F32,
                           precision=lax.Precision.HIGHEST)


def _sigmoid(x):
    return 1.0 / (1.0 + jnp.exp(-x))


def _silu(x):
    return x * _sigmoid(x)


def _softplus(x):
    return jnp.maximum(x, 0.0) + jnp.log1p(jnp.exp(-jnp.abs(x)))


def _log_sigmoid(x):
    return -_softplus(-x)


def _iota(shape, dim):
    return lax.broadcasted_iota(jnp.int32, shape, dim)


def _tri(n):
    return (_iota((n, n), 0) >= _iota((n, n), 1)).astype(F32)


def _eye(n, m):
    return (_iota((n, m), 0) == _iota((n, m), 1)).astype(F32)


def _conv_silu(x, halo, w):
    n = x.shape[0]
    xx = jnp.concatenate([halo, x], axis=0)
    y = w[CONV_K - 1:CONV_K] * x
    for j in range(1, CONV_K):
        y = y + w[CONV_K - 1 - j:CONV_K - j] * pltpu.roll(xx, j, 0)[8:8 + n]
    return _silu(y)


def _cparams(sem):
    return pltpu.CompilerParams(dimension_semantics=sem, vmem_limit_bytes=VMEM_LIMIT)


def _pad_cols(w, mult=LANES):
    pad = (-w.shape[1]) % mult
    return jnp.pad(w, ((0, 0), (0, pad))) if pad else w


def _lane_vec(vals, offset):
    return jnp.zeros((1, LANES), F32).at[0, offset:offset + vals.shape[0]].set(vals.astype(F32))


def _const_spec(shape):
    return pl.BlockSpec(shape, lambda *_: (0,) * len(shape), pipeline_mode=pl.Buffered(1))


def _residual_ln(xb, og, w_out, g, b):
    y = ALPHA * xb + _mm(og, w_out)
    mu = jnp.mean(y, axis=-1, keepdims=True)
    yc = y - mu
    var = jnp.mean(yc * yc, axis=-1, keepdims=True)
    return yc * lax.rsqrt(var + LN_EPS) * g + b


def _out_ln_kernel(og_ref, x_ref, w_ref, g_ref, b_ref, o_ref):
    o_ref[...] = _residual_ln(x_ref[...], og_ref[...], w_ref[...], g_ref[...], b_ref[...])


def _out_ln(og, x2, w, g, b):
    n, d = x2.shape
    wd = og.shape[1]
    tm = 512
    return pl.pallas_call(
        _out_ln_kernel,
        grid=(n // tm,),
        in_specs=[pl.BlockSpec((tm, wd), lambda i: (i, 0)),
                  pl.BlockSpec((tm, d), lambda i: (i, 0)),
                  pl.BlockSpec((wd, d), lambda i: (0, 0)),
                  pl.BlockSpec((1, d), lambda i: (0, 0)),
                  pl.BlockSpec((1, d), lambda i: (0, 0))],
        out_specs=pl.BlockSpec((tm, d), lambda i: (i, 0)),
        out_shape=jax.ShapeDtypeStruct((n, d), F32),
        compiler_params=_cparams(("parallel",)),
        name="out_proj_ln",
    )(og, x2, w, g.reshape(1, d).astype(F32), b.reshape(1, d).astype(F32))


GDN_TB = 256


def _gdn_kernel(x_ref, win_ref, cw_ref, alog_ref, dtb_ref, ng_ref, wout_ref, lg_ref, lb_ref,
                o_ref, s_ref, halo_ref, og_ref):
    j = pl.program_id(1)
    c, tb = CHUNK, GDN_TB
    nch = tb // c
    probs = [(ci, h) for ci in range(nch) for h in range(GDN_H)]
    cw = 3 * GDN_H * 128

    @pl.when(j == 0)
    def _():
        s_ref[...] = jnp.zeros_like(s_ref)
        halo_ref[...] = jnp.zeros_like(halo_ref)

    xb = x_ref[...]
    p = _mm(xb, win_ref[...])

    def conv_tile(col):
        sl = slice(col, col + LANES)
        return _conv_silu(p[:, sl], halo_ref[:, sl], cw_ref[:, sl])

    z = p[:, cw:cw + GDN_H * GDN_DV]

    gat = p[:, cw + GDN_H * GDN_DV:cw + GDN_H * GDN_DV + LANES]
    beta = _sigmoid(gat)
    g = -jnp.exp(alog_ref[...]) * _softplus(gat + dtb_ref[...])
    rt, ct = _iota((tb, tb), 0), _iota((tb, tb), 1)
    tri_bd = ((rt >= ct) & (rt // c == ct // c)).astype(F32)
    gc = _mm_f32(tri_bd, g)
    gct = _mm_nt_f32(_eye(LANES, LANES), gc)

    row = _iota((c, c), 0)
    col = _iota((c, c), 1)
    causal = row >= col
    strict = row > col
    ng = ng_ref[...]
    qoff, koff, voff = 0, GDN_H * GDN_DK, 2 * GDN_H * GDN_DK

    kn, kb, qn, qd, kd, rhs, gl = {}, {}, {}, {}, {}, {}, {}
    for h in range(GDN_H):
        q = conv_tile(qoff + h * 128)
        k = conv_tile(koff + h * 128)
        v = conv_tile(voff + h * 128)
        q = q * (lax.rsqrt(jnp.sum(q * q, axis=-1, keepdims=True) + EPS) * GDN_DK ** -0.5)
        k = k * lax.rsqrt(jnp.sum(k * k, axis=-1, keepdims=True) + EPS)
        beta_c = beta[:, h:h + 1]
        gc_c = jnp.broadcast_to(gc[:, 8 + h:9 + h], (tb, LANES))
        egc = jnp.exp(gc_c)
        g_last = jnp.concatenate(
            [jnp.broadcast_to(gc_c[(ci + 1) * c - 1:(ci + 1) * c, :], (c, LANES))
             for ci in range(nch)], axis=0)
        kbh = k * beta_c
        kn[h], kb[h], qn[h] = k, kbh, q
        qd[h] = q * egc
        kd[h] = k * jnp.exp(g_last - gc_c)
        rhs[h] = jnp.concatenate([v * beta_c, kbh * egc], axis=1)
        gl[h] = g_last
    halo_ref[...] = p[tb - 8:tb, :cw]

    def rows(a, ci):
        return a[ci * c:(ci + 1) * c]

    neg_l, qk = {}, {}
    for ci, h in probs:
        gmat = _mm_nt(jnp.concatenate([rows(kb[h], ci), rows(qn[h], ci)], axis=0), rows(kn[h], ci))
        gc_c = jnp.broadcast_to(gc[ci * c:(ci + 1) * c, 8 + h:9 + h], (c, c))
        gc_r = gct[8 + h:9 + h, ci * c:(ci + 1) * c]
        decay = jnp.exp(jnp.where(causal, gc_c - gc_r, -jnp.inf))
        neg_l[ci, h] = jnp.where(strict, -(gmat[:c] * decay), 0.0)
        qk[ci, h] = jnp.where(causal, gmat[c:] * decay, 0.0)
    pw = dict(neg_l)
    m = {key: _mm(neg_l[key], neg_l[key]) for key in probs}
    nsq = int(math.log2(c)) - 1
    for it in range(nsq):
        if it < nsq - 1:
            r = {key: _mm(jnp.concatenate([pw[key], m[key]], axis=0), m[key]) for key in probs}
            pw = {key: pw[key] + m[key] + r[key][:c] for key in probs}
            m = {key: r[key][c:] for key in probs}
        else:
            pw = {key: pw[key] + m[key] + _mm(pw[key], m[key]) for key in probs}
    uw = {}
    for ci, h in probs:
        rh = rows(rhs[h], ci)
        uw[ci, h] = rh + _mm(pw[ci, h], rh)

    s = [s_ref[h] for h in range(GDN_H)]
    for ci in range(nch):
        t = [_mm(jnp.concatenate([uw[ci, h][:, 128:], rows(qd[h], ci)], axis=0), s[h])
             for h in range(GDN_H)]
        v_new = [uw[ci, h][:, :128] - t[h][:c] for h in range(GDN_H)]
        o = [t[h][c:] + _mm(qk[ci, h], v_new[h]) for h in range(GDN_H)]
        s = [s[h] * jnp.exp(gl[h][ci * c:ci * c + 1, :]) + _mm_tn(rows(kd[h], ci), v_new[h])
             for h in range(GDN_H)]
        for h in range(GDN_H):
            oh = o[h] * lax.rsqrt(jnp.mean(o[h] * o[h], axis=-1, keepdims=True) + EPS) * ng
            zt = z[ci * c:(ci + 1) * c, h * 128:(h + 1) * 128]
            og_ref[ci * c:(ci + 1) * c, h * 128:(h + 1) * 128] = (oh * _silu(zt)).astype(og_ref.dtype)
    for h in range(GDN_H):
        s_ref[h] = s[h]
    o_ref[...] = _residual_ln(xb, og_ref[...], wout_ref[...], lg_ref[...], lb_ref[...])


def _gdn_layer(x2, bsz, t_len, w_in, conv_w, a_log, dt_bias, norm_g, w_out, ln_g, ln_b):
    n, d = x2.shape
    tb = GDN_TB
    nb = t_len // tb
    cw = 3 * GDN_H * 128
    w_in = _pad_cols(w_in.astype(BF16))
    row = lambda b, j: (b * nb + j, 0)
    return pl.pallas_call(
        _gdn_kernel,
        grid=(bsz, nb),
        in_specs=[
            pl.BlockSpec((tb, d), row),
            _const_spec(w_in.shape),
            _const_spec((CONV_K, cw)),
            _const_spec((1, LANES)),
            _const_spec((1, LANES)),
            _const_spec((1, LANES)),
            _const_spec(w_out.shape),
            _const_spec((1, d)),
            _const_spec((1, d)),
        ],
        out_specs=pl.BlockSpec((tb, d), row),
        out_shape=jax.ShapeDtypeStruct((n, d), F32),
        scratch_shapes=[pltpu.VMEM((GDN_H, GDN_DK, GDN_DV), F32),
                        pltpu.VMEM((8, cw), F32),
                        pltpu.VMEM((tb, GDN_H * GDN_DV), BF16)],
        compiler_params=_cparams(("arbitrary", "arbitrary")),
        name="gdn_layer",
    )(x2, w_in, conv_w.astype(F32), _lane_vec(a_log, 8), _lane_vec(dt_bias, 8),
      norm_g.reshape(1, GDN_DV).astype(F32), w_out.astype(BF16),
      ln_g.reshape(1, d).astype(F32), ln_b.reshape(1, d).astype(F32))


def _rope_tile(x, cos, sin_signed, first_half):
    rot = jnp.where(first_half, pltpu.roll(x, LANES - SWA_HD // 2, 1), pltpu.roll(x, SWA_HD // 2, 1))
    return x * cos + rot * sin_signed


SWA_TB = 256


def _swa_kernel(sink_ref, x_ref, win_ref, cos_ref, sin_ref, wout_ref, lg_ref, lb_ref,
                o_ref, kp_ref, vp_ref, og_ref):
    j = pl.program_id(1)
    w, tb = SWA_W, SWA_TB
    nw = tb // w
    q_w, kv_w = SWA_H * SWA_HD, SWA_KVH * SWA_HD

    @pl.when(j == 0)
    def _():
        kp_ref[...] = jnp.zeros_like(kp_ref)
        vp_ref[...] = jnp.zeros_like(vp_ref)

    xb = x_ref[...]
    p = _mm(xb, win_ref[...])
    z = p[:, q_w + 2 * kv_w:]
    cos = cos_ref[...]
    sin = sin_ref[...]
    lane = _iota((w, LANES), 1)
    lo = lane < SWA_HD
    lane_t = _iota((tb, LANES), 1)
    first_half = (lane_t % SWA_HD) < SWA_HD // 2
    lo2 = _iota((2 * w, LANES), 1) < SWA_HD

    kr = _rope_tile(p[:, q_w:q_w + kv_w], cos, sin, first_half)
    vc = p[:, q_w + kv_w:q_w + 2 * kv_w]
    kall = jnp.concatenate([kp_ref[...], kr], axis=0)
    vall = jnp.concatenate([vp_ref[...], vc], axis=0)
    kp_ref[...] = kr[tb - w:]
    vp_ref[...] = vc[tb - w:]

    r = _iota((2 * w, 2 * w), 0) % w
    cidx = _iota((2 * w, 2 * w), 1)
    in_prev = (cidx < w) & (cidx > r)
    in_cur = (cidx >= w) & (cidx - w <= r)
    top = _iota((2 * w, 1), 0) < w

    group = SWA_H // SWA_KVH
    tiles = [(wi, t) for wi in range(nw) for t in range(SWA_H // 2)]
    kdup, vdup = {}, {}
    for wi in range(nw):
        kcat = kall[wi * w:(wi + 2) * w]
        vcat = vall[wi * w:(wi + 2) * w]
        kswap = pltpu.roll(kcat, SWA_HD, 1)
        vswap = pltpu.roll(vcat, SWA_HD, 1)
        for kvh in range(SWA_KVH):
            own = lo2 if kvh == 0 else jnp.logical_not(lo2)
            kdup[wi, kvh] = jnp.where(own, kcat, kswap)
            vdup[wi, kvh] = jnp.where(own, vcat, vswap)
    qr = {}
    for t in range(SWA_H // 2):
        qr[t] = _rope_tile(p[:, t * LANES:(t + 1) * LANES], cos, sin, first_half) * SWA_HD ** -0.5
    scores = {}
    for wi, t in tiles:
        qt = qr[t][wi * w:(wi + 1) * w]
        q2 = jnp.concatenate([jnp.where(lo, qt, 0.0), jnp.where(lo, 0.0, qt)], axis=0)
        valid = ((in_prev & (j > 0)) | in_cur) if wi == 0 else (in_prev | in_cur)
        scores[wi, t] = jnp.where(valid, _mm_nt(q2, kdup[wi, t // (group // 2)]), -jnp.inf)
    probs = {}
    for wi, t in tiles:
        s = scores[wi, t]
        sk = jnp.where(top, sink_ref[2 * t], sink_ref[2 * t + 1])
        m = jnp.maximum(jnp.max(s, axis=-1, keepdims=True), sk)
        e = jnp.exp(s - m)
        den = jnp.sum(e, axis=-1, keepdims=True) + jnp.exp(sk - m)
        probs[wi, t] = e / den
    for wi, t in tiles:
        o2 = _mm(probs[wi, t], vdup[wi, t // (group // 2)])
        ot = jnp.where(lo, o2[:w], o2[w:])
        zt = z[wi * w:(wi + 1) * w, t * LANES:(t + 1) * LANES]
        og_ref[wi * w:(wi + 1) * w, t * LANES:(t + 1) * LANES] = (ot * _silu(zt)).astype(og_ref.dtype)
    o_ref[...] = _residual_ln(xb, og_ref[...], wout_ref[...], lg_ref[...], lb_ref[...])


def _swa_layer(x2, bsz, t_len, w_in, sinks, w_out, ln_g, ln_b):
    n, d = x2.shape
    tb = SWA_TB
    nb = t_len // tb
    q_w = SWA_H * SWA_HD
    cos_t, sin_t = _rope_tables(t_len)
    row = lambda b, j: (b * nb + j, 0)
    return pl.pallas_call(
        _swa_kernel,
        grid=(bsz, nb),
        in_specs=[
            pl.BlockSpec(memory_space=pltpu.SMEM),
            pl.BlockSpec((tb, d), row),
            _const_spec(w_in.shape),
            pl.BlockSpec((tb, LANES), lambda b, j: (j, 0)),
            pl.BlockSpec((tb, LANES), lambda b, j: (j, 0)),
            _const_spec(w_out.shape),
            _const_spec((1, d)),
            _const_spec((1, d)),
        ],
        out_specs=pl.BlockSpec((tb, d), row),
        out_shape=jax.ShapeDtypeStruct((n, d), F32),
        scratch_shapes=[pltpu.VMEM((SWA_W, LANES), F32), pltpu.VMEM((SWA_W, LANES), F32),
                        pltpu.VMEM((tb, q_w), BF16)],
        compiler_params=_cparams(("arbitrary", "arbitrary")),
        name="swa_layer",
    )(sinks.astype(F32), x2, w_in.astype(BF16), cos_t, sin_t, w_out.astype(BF16),
      ln_g.reshape(1, d).astype(F32), ln_b.reshape(1, d).astype(F32))


def _rope_tables(t_len):
    half = SWA_HD // 2
    inv = ROPE_THETA ** (-jnp.arange(half, dtype=F32) / half)
    ang = jnp.arange(t_len, dtype=F32)[:, None] * inv[None, :]
    cos, sin = jnp.cos(ang), jnp.sin(ang)
    return (jnp.concatenate([cos, cos, cos, cos], axis=1),
            jnp.concatenate([-sin, sin, -sin, sin], axis=1))


FOX_BIAS_LANES = 3


def _split3(x):
    x1 = x.astype(BF16).astype(F32)
    r1 = x - x1
    x2 = r1.astype(BF16).astype(F32)
    x3 = (r1 - x2).astype(BF16).astype(F32)
    return x1, x2, x3


FOX_TB = 256
FOX_VR = FOX_HD + 16


def _fox_prep_kernel(x_ref, win_ref, fb_ref, qg_ref, kg_ref,
                     qa_ref, ka_ref, vt_ref, z_ref, carry_ref):
    j = pl.program_id(1)
    tb = FOX_TB
    wd = FOX_H * FOX_HD

    @pl.when(j == 0)
    def _():
        carry_ref[...] = jnp.zeros_like(carry_ref)

    p = _mm(x_ref[...], win_ref[...])
    z_ref[...] = p[:, 3 * wd:4 * wd]
    logf = _log_sigmoid(p[:, 4 * wd:4 * wd + LANES] + fb_ref[...])
    cum = _mm_f32(_tri(tb), logf) + carry_ref[...]
    carry_ref[...] = cum[tb - 1:tb, :]
    cum2 = cum * LOG2E

    lane = _iota((tb, LANES), 1)
    ones2 = ((_iota((LANES, LANES), 0) // FOX_HD) == (_iota((LANES, LANES), 1) // FOX_HD)).astype(BF16)

    def norm(x, g):
        x2 = x * x
        hi = x2.astype(BF16)
        lo = (x2 - hi.astype(F32)).astype(BF16)
        ss = (jnp.dot(hi, ones2, preferred_element_type=F32)
              + jnp.dot(lo, ones2, preferred_element_type=F32))
        return x * lax.rsqrt(ss * (1.0 / FOX_HD) + EPS) * g

    head_lane = lane < FOX_HD
    bl = lane - FOX_HD
    ones_rows = (_iota((FOX_VR - FOX_HD, tb), 0) == 0).astype(BF16)
    cum_split = _split3(cum2)
    ones_q = ((bl >= 3) & (bl < 6)).astype(F32)
    ones_k = ((bl >= 0) & (bl < 3)).astype(F32)
    for t in range(FOX_H // 2):
        sl = slice(t * LANES, (t + 1) * LANES)
        qn = norm(p[:, t * LANES:(t + 1) * LANES], qg_ref[...]) * (FOX_HD ** -0.5 * LOG2E)
        kn = norm(p[:, wd + t * LANES:wd + (t + 1) * LANES], kg_ref[...])
        vt2 = p[:, 2 * wd + t * LANES:2 * wd + (t + 1) * LANES].T.astype(BF16)
        for half in range(2):
            h = 2 * t + half
            c1, c2, c3 = (jnp.broadcast_to(cs[:, h:h + 1], (tb, LANES)) for cs in cum_split)
            qb = jnp.where(bl == 0, c1, jnp.where(bl == 1, c2, jnp.where(bl == 2, c3, ones_q)))
            kb = jnp.where(bl == 3, -c1, jnp.where(bl == 4, -c2, jnp.where(bl == 5, -c3, ones_k)))
            qh = qn if half == 0 else pltpu.roll(qn, FOX_HD, 1)
            kh = kn if half == 0 else pltpu.roll(kn, FOX_HD, 1)
            qa_ref[:, h * LANES:(h + 1) * LANES] = jnp.where(head_lane, qh, qb).astype(BF16)
            ka_ref[:, h * LANES:(h + 1) * LANES] = jnp.where(head_lane, kh, kb).astype(BF16)
            vt_ref[h * FOX_VR:h * FOX_VR + FOX_HD, :] = vt2[half * FOX_HD:(half + 1) * FOX_HD]
            vt_ref[h * FOX_VR + FOX_HD:(h + 1) * FOX_VR, :] = ones_rows


def _fox_prep(x2, bsz, t_len, w_in, f_bias, q_norm_g, k_norm_g):
    n, d = x2.shape
    tb = FOX_TB
    nb = t_len // tb
    wd = FOX_H * FOX_HD
    w_in = _pad_cols(w_in.astype(BF16))
    row = lambda b, j: (b * nb + j, 0)
    g2 =lambda g: jnp.concatenate([g, g]).reshape(1, LANES).astype(F32)
    return pl.pallas_call(
        _fox_prep_kernel,
        grid=(bsz, nb),
        in_specs=[
            pl.BlockSpec((tb, d), row),
            _const_spec(w_in.shape),
            _const_spec((1, LANES)),
            _const_spec((1, LANES)),
            _const_spec((1, LANES)),
        ],
        out_specs=[
            pl.BlockSpec((tb, 2 * wd), row),
            pl.BlockSpec((tb, 2 * wd), row),
            pl.BlockSpec((FOX_H * FOX_VR, tb), lambda b, j: (b, j)),
            pl.BlockSpec((tb, wd), row),
        ],
        out_shape=[jax.ShapeDtypeStruct((n, 2 * wd), BF16),
                   jax.ShapeDtypeStruct((n, 2 * wd), BF16),
                   jax.ShapeDtypeStruct((bsz * FOX_H * FOX_VR, t_len), BF16),
                   jax.ShapeDtypeStruct((n, wd), F32)],
        scratch_shapes=[pltpu.VMEM((1, LANES), F32)],
        compiler_params=_cparams(("arbitrary", "arbitrary")),
        name="fox_prep",
    )(x2, w_in, _lane_vec(f_bias, 0), g2(q_norm_g), g2(k_norm_g))


FOX_TQ = 512
FOX_HPS = 4
FOX_KC = 16


def _fox_flash_kernel(qa_ref, ka_ref, vt_ref, z_ref, o_ref, s_scr, e_scr, m_scr, acc_scr):
    i = pl.program_id(2)
    tq, nh = FOX_TQ, FOX_HPS
    m_scr[...] = jnp.full(m_scr.shape, -jnp.inf, F32)
    acc_scr[...] = jnp.zeros_like(acc_scr)
    q = [qa_ref[:, hh * LANES:(hh + 1) * LANES] for hh in range(nh)]
    causal_c = _iota((FOX_KC, tq), 0) - _iota((FOX_KC, tq), 1)

    def qk(jb, slot):
        start = pl.multiple_of(jb * tq, tq)
        for hh in range(nh):
            s_scr[slot, hh] = lax.dot_general(
                ka_ref[pl.ds(start, tq), hh * LANES:(hh + 1) * LANES], q[hh],
                (((1,), (1,)), ((), ())), preferred_element_type=F32)

    def softmax_pv(jb, slot, masked):
        start = pl.multiple_of(jb * tq, tq)

        def scores(hh, c):
            s = s_scr[slot, hh, c * FOX_KC:(c + 1) * FOX_KC, :]
            return jnp.where(causal_c <= -c * FOX_KC, s, -jnp.inf) if masked else s

        nkc = tq // FOX_KC
        for hh in range(nh):
            m_prev = m_scr[hh]
            mx = functools.reduce(jnp.maximum, [scores(hh, c) for c in range(nkc)])
            m_new = jnp.maximum(m_prev, jnp.max(mx, axis=0, keepdims=True))
            a = jnp.exp2(m_prev - m_new)
            for c in range(nkc):
                e_scr[hh, c * FOX_KC:(c + 1) * FOX_KC, :] = jnp.exp2(scores(hh, c) - m_new).astype(BF16)
            m_scr[hh] = m_new
            vt = vt_ref[hh * FOX_VR:(hh + 1) * FOX_VR, pl.ds(start, tq)]
            acc_scr[hh] = a * acc_scr[hh] + jnp.dot(vt, e_scr[hh], preferred_element_type=F32)

    qk(0, 0)

    def body(t, carry):
        qk(2 * t + 1, 1)
        softmax_pv(2 * t, 0, False)
        qk(2 * t + 2, 0)
        softmax_pv(2 * t + 1, 1, False)
        return carry

    lax.fori_loop(0, i // 2, body, 0)

    @pl.when(i % 2 == 0)
    def _():
        softmax_pv(i, 0, True)

    @pl.when(i % 2 == 1)
    def _():
        qk(i, 1)
        softmax_pv(i - 1, 0, False)
        softmax_pv(i, 1, True)

    for t in range(nh // 2):
        ot = jnp.concatenate([acc_scr[hh, :FOX_HD] / acc_scr[hh, FOX_HD:FOX_HD + 1]
                              for hh in (2 * t, 2 * t + 1)], axis=0)
        sl = slice(t * LANES, (t + 1) * LANES)
        o_ref[:, sl] = (ot.T * _silu(z_ref[:, sl])).astype(o_ref.dtype)


def _fox_flash(qa, ka, vt, z, bsz, t_len):
    n = qa.shape[0]
    tq = FOX_TQ
    nq = t_len // tq
    wd = FOX_H * FOX_HD
    nh = FOX_HPS
    ng = FOX_H // nh
    return pl.pallas_call(
        _fox_flash_kernel,
        grid=(bsz, ng, nq),
        in_specs=[
            pl.BlockSpec((tq, nh * LANES), lambda b, hg, i: (b * nq + i, hg)),
            pl.BlockSpec((t_len, nh * LANES), lambda b, hg, i: (b, hg)),
            pl.BlockSpec((nh * FOX_VR, t_len), lambda b, hg, i: (b * ng + hg, 0)),
            pl.BlockSpec((tq, nh // 2 * LANES), lambda b, hg, i: (b * nq + i, hg)),
        ],
        out_specs=pl.BlockSpec((tq, nh // 2 * LANES), lambda b, hg, i: (b * nq + i, hg)),
        out_shape=jax.ShapeDtypeStruct((n, wd), BF16),
        scratch_shapes=[pltpu.VMEM((2, nh, tq, tq), F32), pltpu.VMEM((nh, tq, tq), BF16),
                        pltpu.VMEM((nh, 1, tq), F32), pltpu.VMEM((nh, FOX_VR, tq), F32)],
        compiler_params=_cparams(("parallel", "parallel", "arbitrary")),
        name="fox_flash",
    )(qa, ka, vt, z)


ML_TB = 256


def _mlstm_kernel(x_ref, win_ref, cw_ref, gb_ref, ng_ref, wout_ref, lg_ref, lb_ref,
                  o_ref, c_ref, m_ref, halo_ref, og_ref):
    j = pl.program_id(1)
    c, tb = CHUNK, ML_TB
    nch = tb // c
    probs = [(ci, h) for ci in range(nch) for h in range(ML_H)]
    qkw, vw = 2 * ML_H * ML_DQK, ML_H * ML_DV

    @pl.when(j == 0)
    def _():
        c_ref[...] = jnp.zeros_like(c_ref)
        m_ref[...] = jnp.zeros_like(m_ref)
        halo_ref[...] = jnp.zeros_like(halo_ref)

    xb = x_ref[...]
    p = _mm(xb, win_ref[...])
    y = [_conv_silu(p[:, t * LANES:(t + 1) * LANES], halo_ref[:, t * LANES:(t + 1) * LANES],
                    cw_ref[:, t * LANES:(t + 1) * LANES]) for t in range(qkw // LANES)]
    halo_ref[...] = p[tb - 8:tb, :qkw]
    v_all = p[:, qkw:qkw + vw]
    o_gate = p[:, qkw + vw:qkw + 2 * vw]
    z = p[:, qkw + 2 * vw:qkw + 3 * vw]

    gt = p[:, qkw + 3 * vw:qkw + 3 * vw + LANES] + gb_ref[...]
    lane = _iota((tb, LANES), 1)
    rt, ct = _iota((tb, tb), 0), _iota((tb, tb), 1)
    tri_bd = ((rt >= ct) & (rt // c == ct // c)).astype(F32)
    bsum = _mm_f32(tri_bd, _log_sigmoid(gt))
    rows_t = _mm_nt_f32(_eye(LANES, LANES), jnp.where(lane < ML_H, gt, bsum))

    causal = _iota((c, c), 0) >= _iota((c, c), 1)
    lo = lane < ML_DQK
    ones_col = (lane == 0).astype(F32)
    ng = ng_ref[...]
    kbase = ML_H * ML_DQK

    def rows(a, ci):
        return a[ci * c:(ci + 1) * c]

    q, k, v_aug = {}, {}, {}
    e_d, w_inter, e_mt, kw, w_old = {}, {}, {}, {}, {}
    for h in range(ML_H):
        t, half = h // 2, h % 2
        own = lo if half == 0 else jnp.logical_not(lo)
        q[h] = jnp.where(own, y[t], 0.0) * ML_DQK ** -0.5
        k[h] = jnp.where(own, y[kbase // LANES + t], 0.0)
        v_aug[h] = jnp.concatenate([v_all[:, h * ML_DV:(h + 1) * ML_DV], ones_col], axis=1)
        b_c = jnp.broadcast_to(bsum[:, ML_H + h:ML_H + h + 1], (tb, LANES))
        ig_c = jnp.broadcast_to(gt[:, h:h + 1], (tb, LANES))
        m_st = m_ref[h]
        for ci in range(nch):
            bc, igc = rows(b_c, ci), rows(ig_c, ci)
            b_r = rows_t[ML_H + h:ML_H + h + 1, ci * c:(ci + 1) * c]
            ig_r = rows_t[h:h + 1, ci * c:(ci + 1) * c]
            b_last = bc[c - 1:c, :]
            d = jnp.where(causal, bc[:, :c] - b_r + ig_r, -jnp.inf)
            m_intra = jnp.max(d, axis=-1, keepdims=True)
            a_end = b_last - bc + igc
            m_end = jnp.max(a_end, axis=0, keepdims=True)
            inter = bc + m_st
            m_t = jnp.maximum(inter, m_intra)
            m_new = jnp.maximum(b_last + m_st, m_end)
            e_d[ci, h] = jnp.exp(d - m_t[:, :c])
            w_inter[ci, h] = jnp.exp(inter - m_t)
            e_mt[ci, h] = jnp.exp(-m_t)
            w_old[ci, h] = jnp.exp(b_last + m_st - m_new)
            kw[ci, h] = rows(k[h], ci) * jnp.exp(a_end - m_new)
            m_st = m_new
        m_ref[h] = m_st

    sc = {(ci, h): _mm_nt(rows(q[h], ci), rows(k[h], ci)) for ci, h in probs}
    kv = {(ci, h): _mm_tn(kw[ci, h], rows(v_aug[h], ci)) for ci, h in probs}

    cst = [c_ref[h] for h in range(ML_H)]
    for ci in range(nch):
        qc = [_mm(rows(q[h], ci), cst[h]) for h in range(ML_H)]
        pv = [_mm(sc[ci, h] * e_d[ci, h], rows(v_aug[h], ci)) for h in range(ML_H)]
        for h in range(ML_H):
            wi, wo = w_inter[ci, h], w_old[ci, h]
            num = jnp.concatenate([wi, wi], axis=1) * qc[h] + pv[h]
            cst[h] = jnp.concatenate([wo, wo], axis=1) * cst[h] + kv[ci, h]
            den = num[:, ML_DV:ML_DV + 1]
            hv = num[:, :ML_DV] / jnp.maximum(jnp.abs(den), e_mt[ci, h])
            hv = hv * lax.rsqrt(jnp.mean(hv * hv, axis=-1, keepdims=True) + EPS) * ng
            rs, sl = slice(ci * c, (ci + 1) * c), slice(h * ML_DV, (h + 1) * ML_DV)
            og_ref[rs, sl] = (hv * _sigmoid(o_gate[rs, sl]) * _silu(z[rs, sl])).astype(og_ref.dtype)
    for h in range(ML_H):
        c_ref[h] = cst[h]
    o_ref[...] = _residual_ln(xb, og_ref[...], wout_ref[...], lg_ref[...], lb_ref[...])


def _mlstm_layer(x2, bsz, t_len, w_in, conv_w, gate_bias, norm_g, w_out, ln_g, ln_b):
    n, d = x2.shape
    tb = ML_TB
    nb = t_len // tb
    qkw = 2 * ML_H * ML_DQK
    w_in = _pad_cols(w_in.astype(BF16))
    row = lambda b, j: (b * nb + j, 0)
    return pl.pallas_call(
        _mlstm_kernel,
        grid=(bsz, nb),
        in_specs=[
            pl.BlockSpec((tb, d), row),
            _const_spec(w_in.shape),
            _const_spec((CONV_K, qkw)),
            _const_spec((1, LANES)),
            _const_spec((1, LANES)),
            _const_spec(w_out.shape),
            _const_spec((1, d)),
            _const_spec((1, d)),
        ],
        out_specs=pl.BlockSpec((tb, d), row),
        out_shape=jax.ShapeDtypeStruct((n, d), F32),
        scratch_shapes=[pltpu.VMEM((ML_H, LANES, 2 * ML_DV), F32),
                        pltpu.VMEM((ML_H, 1, LANES), F32),
                        pltpu.VMEM((8, qkw), F32),
                        pltpu.VMEM((tb, ML_H * ML_DV), BF16)],
        compiler_params=_cparams(("arbitrary", "arbitrary")),
        name="mlstm_layer",
    )(x2, w_in, conv_w.astype(F32), _lane_vec(gate_bias, 0), norm_g.reshape(1, ML_DV).astype(F32),
      w_out.astype(BF16), ln_g.reshape(1, d).astype(F32), ln_b.reshape(1, d).astype(F32))


def _fox_layer(x2, bsz, t_len, w_in, f_bias, q_norm_g, k_norm_g, w_out, ln_g, ln_b):
    qa, ka, vt, z = _fox_prep(x2, bsz, t_len, w_in, f_bias, q_norm_g, k_norm_g)
    og = _fox_flash(qa, ka, vt, z, bsz, t_len)
    return _out_ln(og, x2, w_out.astype(BF16), ln_g, ln_b)


def kernel(x, a_w_in, a_conv, a_a_log, a_dt_bias, a_norm_g, a_w_out, b_w_in, b_sinks, b_w_out, c_w_in, c_f_bias, c_q_norm, c_k_norm, c_w_out, d_w_in, d_conv, d_gate_bias, d_norm_g, d_w_out, ln_g, ln_b):
    bsz, t_len, d = x.shape
    x2 = x.reshape(bsz * t_len, d)
    for i in range(DEPTH):
        kind, j = i % 4, i // 4
        if kind == 0:
            x2 = _gdn_layer(x2, bsz, t_len, a_w_in[j], a_conv[j], a_a_log[j], a_dt_bias[j],
                            a_norm_g[j], a_w_out[j], ln_g[i], ln_b[i])
        elif kind == 1:
            x2 = _swa_layer(x2, bsz, t_len, b_w_in[j], b_sinks[j], b_w_out[j], ln_g[i], ln_b[i])
        elif kind == 2:
            x2 = _fox_layer(x2, bsz, t_len, c_w_in[j], c_f_bias[j], c_q_norm[j], c_k_norm[j],
                            c_w_out[j], ln_g[i], ln_b[i])
        else:
            x2 = _mlstm_layer(x2, bsz, t_len, d_w_in[j], d_conv[j], d_gate_bias[j], d_norm_g[j],
                              d_w_out[j], ln_g[i], ln_b[i])
    return x2.reshape(bsz, t_len, d)
```

```python
import functools
import math

import jax
import jax.numpy as jnp
from jax import lax
from jax.experimental import pallas as pl
from jax.experimental.pallas import tpu as pltpu

F32 = jnp.float32
BF16 = jnp.bfloat16
LANES = 128
VMEM_LIMIT = 56 * 1024 * 1024

D_MODEL = 1024
DEPTH = 4
CONV_K = 4
ROPE_THETA = 10000.0
EPS = 1e-6
LN_EPS = 1e-5
ALPHA = (2 * DEPTH) ** 0.25
LOG2E = math.log2(math.e)

GDN_H, GDN_DK, GDN_DV = 8, 128, 128
SWA_H, SWA_KVH, SWA_HD, SWA_W = 16, 2, 64, 128
FOX_H, FOX_HD = 16, 64
ML_H, ML_DQK, ML_DV = 8, 64, 128
CHUNK = 64


def _mm(a, b):
    return jnp.dot(a.astype(BF16), b.astype(BF16), preferred_element_type=F32)


def _mm_nt(a, b):
    return lax.dot_general(a.astype(BF16), b.astype(BF16), (((1,), (1,)), ((), ())),
                           preferred_element_type=F32)


def _mm_tn(a, b):
    return lax.dot_general(a.astype(BF16), b.astype(BF16), (((0,), (0,)), ((), ())),
                           preferred_element_type=F32)


def _mm_f32(a, b):
    return jnp.dot(a, b, preferred_element_type=F32, precision=lax.Precision.HIGHEST)


def _mm_nt_f32(a, b):
    return lax.dot_general(a, b, (((1,), (1,)), ((), ())), preferred_element_type=F32,
                           precision=lax.Precision.HIGHEST)


def _sigmoid(x):
    return 1.0 / (1.0 + jnp.exp(-x))


def _silu(x):
    return x * _sigmoid(x)


def _softplus(x):
    return jnp.maximum(x, 0.0) + jnp.log1p(jnp.exp(-jnp.abs(x)))


def _log_sigmoid(x):
    return -_softplus(-x)


def _iota(shape, dim):
    return lax.broadcasted_iota(jnp.int32, shape, dim)


def _tri(n):
    return (_iota((n, n), 0) >= _iota((n, n), 1)).astype(F32)


def _eye(n, m):
    return (_iota((n, m), 0) == _iota((n, m), 1)).astype(F32)


def _conv_silu(x, halo, w):
    n = x.shape[0]
    xx = jnp.concatenate([halo, x], axis=0)
    y = w[CONV_K - 1:CONV_K] * x
    for j in range(1, CONV_K):
        y = y + w[CONV_K - 1 - j:CONV_K - j] * pltpu.roll(xx, j, 0)[8:8 + n]
    return _silu(y)


def _cparams(sem):
    return pltpu.CompilerParams(dimension_semantics=sem, vmem_limit_bytes=VMEM_LIMIT)


def _pad_cols(w, mult=LANES):
    pad = (-w.shape[1]) % mult
    return jnp.pad(w, ((0, 0), (0, pad))) if pad else w


def _lane_vec(vals, offset):
    return jnp.zeros((1, LANES), F32).at[0, offset:offset + vals.shape[0]].set(vals.astype(F32))


def _const_spec(shape):
    return pl.BlockSpec(shape, lambda *_: (0,) * len(shape), pipeline_mode=pl.Buffered(1))


def _residual_ln(xb, og, w_out, g, b):
    y = ALPHA * xb + _mm(og, w_out)
    mu = jnp.mean(y, axis=-1, keepdims=True)
    yc = y - mu
    var = jnp.mean(yc * yc, axis=-1, keepdims=True)
    return yc * lax.rsqrt(var + LN_EPS) * g + b


def _out_ln_kernel(og_ref, x_ref, w_ref, g_ref, b_ref, o_ref):
    o_ref[...] = _residual_ln(x_ref[...], og_ref[...], w_ref[...], g_ref[...], b_ref[...])


def _out_ln(og, x2, w, g, b):
    n, d = x2.shape
    wd = og.shape[1]
    tm = 512
    return pl.pallas_call(
        _out_ln_kernel,
        grid=(n // tm,),
        in_specs=[pl.BlockSpec((tm, wd), lambda i: (i, 0)),
                  pl.BlockSpec((tm, d), lambda i: (i, 0)),
                  pl.BlockSpec((wd, d), lambda i: (0, 0)),
                  pl.BlockSpec((1, d), lambda i: (0, 0)),
                  pl.BlockSpec((1, d), lambda i: (0, 0))],
        out_specs=pl.BlockSpec((tm, d), lambda i: (i, 0)),
        out_shape=jax.ShapeDtypeStruct((n, d), F32),
        compiler_params=_cparams(("parallel",)),
        name="out_proj_ln",
    )(og, x2, w, g.reshape(1, d).astype(F32), b.reshape(1, d).astype(F32))


GDN_TB = 256
GDN_WAVE = 2


def _gdn_kernel(x_ref, win_ref, cw_ref, alog_ref, dtb_ref, ng_ref, wout_ref, lg_ref, lb_ref,
                o_ref, s_ref, halo_ref, og_ref):
    j = pl.program_id(1)
    c, tb = CHUNK, GDN_TB
    nch = tb // c
    probs = [(ci, h) for ci in range(nch) for h in range(GDN_H)]
    cw = 3 * GDN_H * 128

    @pl.when(j == 0)
    def _():
        s_ref[...] = jnp.zeros_like(s_ref)
        halo_ref[...] = jnp.zeros_like(halo_ref)

    xb = x_ref[...]
    p = _mm(xb, win_ref[...])

    def conv_tile(col):
        sl = slice(col, col + LANES)
        return _conv_silu(p[:, sl], halo_ref[:, sl], cw_ref[:, sl])

    z = p[:, cw:cw + GDN_H * GDN_DV]

    gat = p[:, cw + GDN_H * GDN_DV:cw + GDN_H * GDN_DV + LANES]
    beta = _sigmoid(gat)
    g = -jnp.exp(alog_ref[...]) * _softplus(gat + dtb_ref[...])
    rt, ct = _iota((tb, tb), 0), _iota((tb, tb), 1)
    tri_bd = ((rt >= ct) & (rt // c == ct // c)).astype(F32)
    gc = _mm_f32(tri_bd, g)
    gct = _mm_nt_f32(_eye(LANES, LANES), gc)

    row = _iota((c, c), 0)
    col = _iota((c, c), 1)
    causal = row >= col
    strict = row > col
    ng = ng_ref[...]
    qoff, koff, voff = 0, GDN_H * GDN_DK, 2 * GDN_H * GDN_DK

    def rows(a, ci):
        return a[ci * c:(ci + 1) * c]

    kn, kb, qn, qd, kd, rhs, gl = {}, {}, {}, {}, {}, {}, {}
    neg_l, qk, m = {}, {}, {}
    for h in range(GDN_H):
        q = conv_tile(qoff + h * 128)
        k = conv_tile(koff + h * 128)
        v = conv_tile(voff + h * 128)
        q = q * (lax.rsqrt(jnp.sum(q * q, axis=-1, keepdims=True) + EPS) * GDN_DK ** -0.5)
        k = k * lax.rsqrt(jnp.sum(k * k, axis=-1, keepdims=True) + EPS)
        beta_c = beta[:, h:h + 1]
        gc_c = jnp.broadcast_to(gc[:, 8 + h:9 + h], (tb, LANES))
        egc = jnp.exp(gc_c)
        g_last = jnp.concatenate(
            [jnp.broadcast_to(gc_c[(ci + 1) * c - 1:(ci + 1) * c, :], (c, LANES))
             for ci in range(nch)], axis=0)
        kbh = k * beta_c
        kn[h], kb[h], qn[h] = k, kbh, q
        qd[h] = q * egc
        kd[h] = k * jnp.exp(g_last - gc_c)
        rhs[h] = jnp.concatenate([v * beta_c, kbh * egc], axis=1)
        gl[h] = g_last
    halo_ref[...] = p[tb - 8:tb, :cw]

    uw, qk = {}, {}

    def wy_stages(keys):
        neg_l = {}
        for ci, h in keys:
            gmat = _mm_nt(jnp.concatenate([rows(kb[h], ci), rows(qn[h], ci)], axis=0), rows(kn[h], ci))
            gc_c = jnp.broadcast_to(gc[ci * c:(ci + 1) * c, 8 + h:9 + h], (c, c))
            gc_r = gct[8 + h:9 + h, ci * c:(ci + 1) * c]
            decay = jnp.exp(jnp.where(causal, gc_c - gc_r, -jnp.inf))
            neg_l[ci, h] = jnp.where(strict, -(gmat[:c] * decay), 0.0)
            qk[ci, h] = jnp.where(causal, gmat[c:] * decay, 0.0)
        yield
        pw = dict(neg_l)
        m = {key: _mm(neg_l[key], neg_l[key]) for key in keys}
        yield
        nsq = int(math.log2(c)) - 1
        for it in range(nsq):
            if it < nsq - 1:
                r = {key: _mm(jnp.concatenate([pw[key], m[key]], axis=0), m[key]) for key in keys}
                pw = {key: pw[key] + m[key] + r[key][:c] for key in keys}
                m = {key: r[key][c:] for key in keys}
            else:
                pw = {key: pw[key] + m[key] + _mm(pw[key], m[key]) for key in keys}
            yield
        for ci, h in keys:
            rh = rows(rhs[h], ci)
            uw[ci, h] = rh + _mm(pw[ci, h], rh)
        yield

    state = {"s": [s_ref[h] for h in range(GDN_H)]}

    def recurrence(chunks):
        for ci in chunks:
            s = state["s"]
            t = [_mm(jnp.concatenate([uw[ci, h][:, 128:], rows(qd[h], ci)], axis=0), s[h])
                 for h in range(GDN_H)]
            yield
            v_new = [uw[ci, h][:, :128] - t[h][:c] for h in range(GDN_H)]
            o = [t[h][c:] + _mm(qk[ci, h], v_new[h]) for h in range(GDN_H)]
            state["s"] = [s[h] * jnp.exp(gl[h][ci * c:ci * c + 1, :])
                          + _mm_tn(rows(kd[h], ci), v_new[h]) for h in range(GDN_H)]
            yield
            for h in range(GDN_H):
                oh = o[h] * lax.rsqrt(jnp.mean(o[h] * o[h], axis=-1, keepdims=True) + EPS) * ng
                zt = z[ci * c:(ci + 1) * c, h * 128:(h + 1) * 128]
                og_ref[ci * c:(ci + 1) * c, h * 128:(h + 1) * 128] = (oh * _silu(zt)).astype(og_ref.dtype)

    groups = [list(range(g, min(g + GDN_WAVE, nch))) for g in range(0, nch, GDN_WAVE)]
    prev = None
    for grp in groups + [None]:
        live = []
        if grp is not None:
            live.append(wy_stages([(ci, h) for ci in grp for h in range(GDN_H)]))
        if prev is not None:
            live.append(recurrence(prev))
        while live:
            for g in list(live):
                if next(g, "done") == "done":
                    live.remove(g)
        prev = grp
    s = state["s"]
    for h in range(GDN_H):
        s_ref[h] = s[h]
    o_ref[...] = _residual_ln(xb, og_ref[...], wout_ref[...], lg_ref[...], lb_ref[...])


def _gdn_layer(x2, bsz, t_len, w_in, conv_w, a_log, dt_bias, norm_g, w_out, ln_g, ln_b):
    n, d = x2.shape
    tb = GDN_TB
    nb = t_len // tb
    cw = 3 * GDN_H * 128
    w_in = _pad_cols(w_in.astype(BF16))
    row = lambda b, j: (b * nb + j, 0)
    return pl.pallas_call(
        _gdn_kernel,
        grid=(bsz, nb),
        in_specs=[
            pl.BlockSpec((tb, d), row),
            _const_spec(w_in.shape),
            _const_spec((CONV_K, cw)),
            _const_spec((1, LANES)),
            _const_spec((1, LANES)),
            _const_spec((1, LANES)),
            _const_spec(w_out.shape),
            _const_spec((1, d)),
            _const_spec((1, d)),
        ],
        out_specs=pl.BlockSpec((tb, d), row),
        out_shape=jax.ShapeDtypeStruct((n, d), F32),
        scratch_shapes=[pltpu.VMEM((GDN_H, GDN_DK, GDN_DV), F32),
                        pltpu.VMEM((8, cw), F32),
                        pltpu.VMEM((tb, GDN_H * GDN_DV), BF16)],
        compiler_params=_cparams(("arbitrary", "arbitrary")),
        name="gdn_layer",
    )(x2, w_in, conv_w.astype(F32), _lane_vec(a_log, 8), _lane_vec(dt_bias, 8),
      norm_g.reshape(1, GDN_DV).astype(F32), w_out.astype(BF16),
      ln_g.reshape(1, d).astype(F32), ln_b.reshape(1, d).astype(F32))


def _rope_tile(x, cos, sin_signed, first_half):
    rot = jnp.where(first_half, pltpu.roll(x, LANES - SWA_HD // 2, 1), pltpu.roll(x, SWA_HD // 2, 1))
    return x * cos + rot * sin_signed


SWA_TB = 256


def _swa_kernel(sink_ref, x_ref, win_ref, cos_ref, sin_ref, wout_ref, lg_ref, lb_ref,
                o_ref, kp_ref, vp_ref, og_ref):
    j = pl.program_id(1)
    w, tb = SWA_W, SWA_TB
    nw = tb // w
    q_w, kv_w = SWA_H * SWA_HD, SWA_KVH * SWA_HD

    @pl.when(j == 0)
    def _():
        kp_ref[...] = jnp.zeros_like(kp_ref)
        vp_ref[...] = jnp.zeros_like(vp_ref)

    xb = x_ref[...]
    p = _mm(xb, win_ref[...])
    z = p[:, q_w + 2 * kv_w:]
    cos = cos_ref[...]
    sin = sin_ref[...]
    lane = _iota((w, LANES), 1)
    lo = lane < SWA_HD
    lane_t = _iota((tb, LANES), 1)
    first_half = (lane_t % SWA_HD) < SWA_HD // 2
    lo2 = _iota((2 * w, LANES), 1) < SWA_HD

    kr = _rope_tile(p[:, q_w:q_w + kv_w], cos, sin, first_half)
    vc = p[:, q_w + kv_w:q_w + 2 * kv_w]
    kall = jnp.concatenate([kp_ref[...], kr], axis=0)
    vall = jnp.concatenate([vp_ref[...], vc], axis=0)
    kp_ref[...] = kr[tb - w:]
    vp_ref[...] = vc[tb - w:]

    r = _iota((2 * w, 2 * w), 0) % w
    cidx = _iota((2 * w, 2 * w), 1)
    in_prev = (cidx < w) & (cidx > r)
    in_cur = (cidx >= w) & (cidx - w <= r)
    top = _iota((2 * w, 1), 0) < w

    group = SWA_H // SWA_KVH
    tiles = [(wi, t) for wi in range(nw) for t in range(SWA_H // 2)]
    kdup, vdup = {}, {}
    for wi in range(nw):
        kcat = kall[wi * w:(wi + 2) * w]
        vcat = vall[wi * w:(wi + 2) * w]
        kswap = pltpu.roll(kcat, SWA_HD, 1)
        vswap = pltpu.roll(vcat, SWA_HD, 1)
        for kvh in range(SWA_KVH):
            own = lo2 if kvh == 0 else jnp.logical_not(lo2)
            kdup[wi, kvh] = jnp.where(own, kcat, kswap)
            vdup[wi, kvh] = jnp.where(own, vcat, vswap)
    qr = {}
    for t in range(SWA_H // 2):
        qr[t] = _rope_tile(p[:, t * LANES:(t + 1) * LANES], cos, sin, first_half) * SWA_HD ** -0.5
    scores = {}
    for wi, t in tiles:
        qt = qr[t][wi * w:(wi + 1) * w]
        q2 = jnp.concatenate([jnp.where(lo, qt, 0.0), jnp.where(lo, 0.0, qt)], axis=0)
        valid = ((in_prev & (j > 0)) | in_cur) if wi == 0 else (in_prev | in_cur)
        scores[wi, t] = jnp.where(valid, _mm_nt(q2, kdup[wi, t // (group // 2)]), -jnp.inf)
    probs = {}
    for wi, t in tiles:
        s = scores[wi, t]
        sk = jnp.where(top, sink_ref[2 * t], sink_ref[2 * t + 1])
        m = jnp.maximum(jnp.max(s, axis=-1, keepdims=True), sk)
        e = jnp.exp(s - m)
        den = jnp.sum(e, axis=-1, keepdims=True) + jnp.exp(sk - m)
        probs[wi, t] = e / den
    for wi, t in tiles:
        o2 = _mm(probs[wi, t], vdup[wi, t // (group // 2)])
        ot = jnp.where(lo, o2[:w], o2[w:])
        zt = z[wi * w:(wi + 1) * w, t * LANES:(t + 1) * LANES]
        og_ref[wi * w:(wi + 1) * w, t * LANES:(t + 1) * LANES] = (ot * _silu(zt)).astype(og_ref.dtype)
    o_ref[...] = _residual_ln(xb, og_ref[...], wout_ref[...], lg_ref[...], lb_ref[...])


def _swa_layer(x2, bsz, t_len, w_in, sinks, w_out, ln_g, ln_b):
    n, d = x2.shape
    tb = SWA_TB
    nb = t_len // tb
    q_w = SWA_H * SWA_HD
    cos_t, sin_t = _rope_tables(t_len)
    row = lambda b, j: (b * nb + j, 0)
    return pl.pallas_call(
        _swa_kernel,
        grid=(bsz, nb),
        in_specs=[
            pl.BlockSpec(memory_space=pltpu.SMEM),
            pl.BlockSpec((tb, d), row),
            _const_spec(w_in.shape),
            pl.BlockSpec((tb, LANES), lambda b, j: (j, 0)),
            pl.BlockSpec((tb, LANES), lambda b, j: (j, 0)),
            _const_spec(w_out.shape),
            _const_spec((1, d)),
            _const_spec((1, d)),
        ],
        out_specs=pl.BlockSpec((tb, d), row),
        out_shape=jax.ShapeDtypeStruct((n, d), F32),
        scratch_shapes=[pltpu.VMEM((SWA_W, LANES), F32), pltpu.VMEM((SWA_W, LANES), F32),
                        pltpu.VMEM((tb, q_w), BF16)],
        compiler_params=_cparams(("arbitrary", "arbitrary")),
        name="swa_layer",
    )(sinks.astype(F32), x2, w_in.astype(BF16), cos_t, sin_t, w_out.astype(BF16),
      ln_g.reshape(1, d).astype(F32), ln_b.reshape(1, d).astype(F32))


def _rope_tables(t_len):
    half = SWA_HD // 2
    inv = ROPE_THETA ** (-jnp.arange(half, dtype=F32) / half)
    ang = jnp.arange(t_len, dtype=F32)[:, None] * inv[None, :]
    cos, sin = jnp.cos(ang), jnp.sin(ang)
    return (jnp.concatenate([cos, cos, cos, cos], axis=1),
            jnp.concatenate([-sin, sin, -sin, sin], axis=1))


FOX_BIAS_LANES = 3


def _split3(x):
    x1 = x.astype(BF16).astype(F32)
    r1 = x - x1
    x2 = r1.astype(BF16).astype(F32)
    x3 = (r1 - x2).astype(BF16).astype(F32)
    return x1, x2, x3


FOX_TB = 256
FOX_VR = FOX_HD + 16


def _fox_prep_kernel(x_ref, win_ref, fb_ref, qg_ref, kg_ref,
                     qa_ref, ka_ref, vt_ref, z_ref, carry_ref):
    j = pl.program_id(1)
    tb = FOX_TB
    wd = FOX_H * FOX_HD

    @pl.when(j == 0)
    def _():
        carry_ref[...] = jnp.zeros_like(carry_ref)

    p = _mm(x_ref[...], win_ref[...])
    z_ref[...] = p[:, 3 * wd:4 * wd]
    logf = _log_sigmoid(p[:, 4 * wd:4 * wd + LANES] + fb_ref[...])
    cum = _mm_f32(_tri(tb), logf) + carry_ref[...]
    carry_ref[...] = cum[tb - 1:tb, :]
    cum2 = cum * LOG2E

    lane = _iota((tb, LANES), 1)
    ones2 = ((_iota((LANES, LANES), 0) // FOX_HD) == (_iota((LANES, LANES), 1) // FOX_HD)).astype(BF16)

    def norm(x, g):
        x2 = x * x
        hi = x2.astype(BF16)
        lo = (x2 - hi.astype(F32)).astype(BF16)
        ss = (jnp.dot(hi, ones2, preferred_element_type=F32)
              + jnp.dot(lo, ones2, preferred_element_type=F32))
        return x * lax.rsqrt(ss * (1.0 / FOX_HD) + EPS) * g

    head_lane = lane < FOX_HD
    bl = lane - FOX_HD
    ones_rows = (_iota((FOX_VR - FOX_HD, tb), 0) == 0).astype(BF16)
    cum_split = _split3(cum2)
    ones_q = ((bl >= 3) & (bl < 6)).astype(F32)
    ones_k = ((bl >= 0) & (bl < 3)).astype(F32)
    for t in range(FOX_H // 2):
        sl = slice(t * LANES, (t + 1) * LANES)
        qn = norm(p[:, t * LANES:(t + 1) * LANES], qg_ref[...]) * (FOX_HD ** -0.5 * LOG2E)
        kn = norm(p[:, wd + t * LANES:wd + (t + 1) * LANES], kg_ref[...])
        vt2 = p[:, 2 * wd + t * LANES:2 * wd + (t + 1) * LANES].T.astype(BF16)
        for half in range(2):
            h = 2 * t + half
            c1, c2, c3 = (jnp.broadcast_to(cs[:, h:h + 1], (tb, LANES)) for cs in cum_split)
            qb = jnp.where(bl == 0, c1, jnp.where(bl == 1, c2, jnp.where(bl == 2, c3, ones_q)))
            kb = jnp.where(bl == 3, -c1, jnp.where(bl == 4, -c2, jnp.where(bl == 5, -c3, ones_k)))
            qh = qn if half == 0 else pltpu.roll(qn, FOX_HD, 1)
            kh = kn if half == 0 else pltpu.roll(kn, FOX_HD, 1)
            qa_ref[:, h * LANES:(h + 1) * LANES] = jnp.where(head_lane, qh, qb).astype(BF16)
            ka_ref[:, h * LANES:(h + 1) * LANES] = jnp.where(head_lane, kh, kb).astype(BF16)
            vt_ref[h * FOX_VR:h * FOX_VR + FOX_HD, :] = vt2[half * FOX_HD:(half + 1) * FOX_HD]
            vt_ref[h * FOX_VR + FOX_HD:(h + 1) * FOX_VR, :] = ones_rows


def _fox_prep(x2, bsz, t_len, w_in, f_bias, q_norm_g, k_norm_g):
    n, d = x2.shape
    tb = FOX_TB
    nb = t_len // tb
    wd = FOX_H * FOX_HD
    w_in = _pad_cols(w_in.astype(BF16))
    row = lambda b, j: (b * nb + j, 0)
    g2 =lambda g: jnp.concatenate([g, g]).reshape(1, LANES).astype(F32)
    return pl.pallas_call(
        _fox_prep_kernel,
        grid=(bsz, nb),
        in_specs=[
            pl.BlockSpec((tb, d), row),
            _const_spec(w_in.shape),
            _const_spec((1, LANES)),
            _const_spec((1, LANES)),
            _const_spec((1, LANES)),
        ],
        out_specs=[
            pl.BlockSpec((tb, 2 * wd), row),
            pl.BlockSpec((tb, 2 * wd), row),
            pl.BlockSpec((FOX_H * FOX_VR, tb), lambda b, j: (b, j)),
            pl.BlockSpec((tb, wd), row),
        ],
        out_shape=[jax.ShapeDtypeStruct((n, 2 * wd), BF16),
                   jax.ShapeDtypeStruct((n, 2 * wd), BF16),
                   jax.ShapeDtypeStruct((bsz * FOX_H * FOX_VR, t_len), BF16),
                   jax.ShapeDtypeStruct((n, wd), F32)],
        scratch_shapes=[pltpu.VMEM((1, LANES), F32)],
        compiler_params=_cparams(("arbitrary", "arbitrary")),
        name="fox_prep",
    )(x2, w_in, _lane_vec(f_bias, 0), g2(q_norm_g), g2(k_norm_g))


FOX_TQ = 512
FOX_HPS = 4
FOX_KC = 16


def _fox_flash_kernel(qa_ref, ka_ref, vt_ref, z_ref, o_ref, s_scr, e_scr, m_scr, acc_scr):
    i = pl.program_id(2)
    tq, nh = FOX_TQ, FOX_HPS
    m_scr[...] = jnp.full(m_scr.shape, -jnp.inf, F32)
    acc_scr[...] = jnp.zeros_like(acc_scr)
    q = [qa_ref[:, hh * LANES:(hh + 1) * LANES] for hh in range(nh)]
    causal_c = _iota((FOX_KC, tq), 0) - _iota((FOX_KC, tq), 1)

    def qk(jb, slot):
        start = pl.multiple_of(jb * tq, tq)
        for hh in range(nh):
            s_scr[slot, hh] = lax.dot_general(
                ka_ref[pl.ds(start, tq), hh * LANES:(hh + 1) * LANES], q[hh],
                (((1,), (1,)), ((), ())), preferred_element_type=F32)

    def softmax_pv(jb, slot, masked):
        start = pl.multiple_of(jb * tq, tq)

        def scores(hh, c):
            s = s_scr[slot, hh, c * FOX_KC:(c + 1) * FOX_KC, :]
            return jnp.where(causal_c <= -c * FOX_KC, s, -jnp.inf) if masked else s

        nkc = tq // FOX_KC
        for hh in range(nh):
            m_prev = m_scr[hh]
            mx = functools.reduce(jnp.maximum, [scores(hh, c) for c in range(nkc)])
            m_new = jnp.maximum(m_prev, jnp.max(mx, axis=0, keepdims=True))
            a = jnp.exp2(m_prev - m_new)
            for c in range(nkc):
                e_scr[hh, c * FOX_KC:(c + 1) * FOX_KC, :] = jnp.exp2(scores(hh, c) - m_new).astype(BF16)
            m_scr[hh] = m_new
            vt = vt_ref[hh * FOX_VR:(hh + 1) * FOX_VR, pl.ds(start, tq)]
            acc_scr[hh] = a * acc_scr[hh] + jnp.dot(vt, e_scr[hh], preferred_element_type=F32)

    qk(0, 0)

    def body(t, carry):
        qk(2 * t + 1, 1)
        softmax_pv(2 * t, 0, False)
        qk(2 * t + 2, 0)
        softmax_pv(2 * t + 1, 1, False)
        return carry

    lax.fori_loop(0, i // 2, body, 0)

    @pl.when(i % 2 == 0)
    def _():
        softmax_pv(i, 0, True)

    @pl.when(i % 2 == 1)
    def _():
        qk(i, 1)
        softmax_pv(i - 1, 0, False)
        softmax_pv(i, 1, True)

    for t in range(nh // 2):
        ot = jnp.concatenate([acc_scr[hh, :FOX_HD] / acc_scr[hh, FOX_HD:FOX_HD + 1]
                              for hh in (2 * t, 2 * t + 1)], axis=0)
        sl = slice(t * LANES, (t + 1) * LANES)
        o_ref[:, sl] = (ot.T * _silu(z_ref[:, sl])).astype(o_ref.dtype)


def _fox_flash(qa, ka, vt, z, bsz, t_len):
    n = qa.shape[0]
    tq = FOX_TQ
    nq = t_len // tq
    wd = FOX_H * FOX_HD
    nh = FOX_HPS
    ng = FOX_H // nh
    return pl.pallas_call(
        _fox_flash_kernel,
        grid=(bsz, ng, nq),
        in_specs=[
            pl.BlockSpec((tq, nh * LANES), lambda b, hg, i: (b * nq + i, hg)),
            pl.BlockSpec((t_len, nh * LANES), lambda b, hg, i: (b, hg)),
            pl.BlockSpec((nh * FOX_VR, t_len), lambda b, hg, i: (b * ng + hg, 0)),
            pl.BlockSpec((tq, nh // 2 * LANES), lambda b, hg, i: (b * nq + i, hg)),
        ],
        out_specs=pl.BlockSpec((tq, nh // 2 * LANES), lambda b, hg, i: (b * nq + i, hg)),
        out_shape=jax.ShapeDtypeStruct((n, wd), BF16),
        scratch_shapes=[pltpu.VMEM((2, nh, tq, tq), F32), pltpu.VMEM((nh, tq, tq), BF16),
                        pltpu.VMEM((nh, 1, tq), F32), pltpu.VMEM((nh, FOX_VR, tq), F32)],
        compiler_params=_cparams(("parallel", "parallel", "arbitrary")),
        name="fox_flash",
    )(qa, ka, vt, z)


ML_TB = 256
ML_WAVE = 2


def _mlstm_kernel(x_ref, win_ref, cw_ref, gb_ref, ng_ref, wout_ref, lg_ref, lb_ref,
                  o_ref, c_ref, m_ref, halo_ref, og_ref):
    j = pl.program_id(1)
    c, tb = CHUNK, ML_TB
    nch = tb // c
    probs = [(ci, h) for ci in range(nch) for h in range(ML_H)]
    qkw, vw = 2 * ML_H * ML_DQK, ML_H * ML_DV

    @pl.when(j == 0)
    def _():
        c_ref[...] = jnp.zeros_like(c_ref)
        m_ref[...] = jnp.zeros_like(m_ref)
        halo_ref[...] = jnp.zeros_like(halo_ref)

    xb = x_ref[...]
    p = _mm(xb, win_ref[...])
    y = [_conv_silu(p[:, t * LANES:(t + 1) * LANES], halo_ref[:, t * LANES:(t + 1) * LANES],
                    cw_ref[:, t * LANES:(t + 1) * LANES]) for t in range(qkw // LANES)]
    halo_ref[...] = p[tb - 8:tb, :qkw]
    v_all = p[:, qkw:qkw + vw]
    o_gate = p[:, qkw + vw:qkw + 2 * vw]
    z = p[:, qkw + 2 * vw:qkw + 3 * vw]

    gt = p[:, qkw + 3 * vw:qkw + 3 * vw + LANES] + gb_ref[...]
    lane = _iota((tb, LANES), 1)
    rt, ct = _iota((tb, tb), 0), _iota((tb, tb), 1)
    tri_bd = ((rt >= ct) & (rt // c == ct // c)).astype(F32)
    bsum = _mm_f32(tri_bd, _log_sigmoid(gt))
    rows_t = _mm_nt_f32(_eye(LANES, LANES), jnp.where(lane < ML_H, gt, bsum))

    causal = _iota((c, c), 0) >= _iota((c, c), 1)
    lo = lane < ML_DQK
    ones_col = (lane == 0).astype(F32)
    ng = ng_ref[...]
    kbase = ML_H * ML_DQK

    def rows(a, ci):
        return a[ci * c:(ci + 1) * c]

    q, k, v_aug = {}, {}, {}
    e_d, w_inter, e_mt, kw, w_old = {}, {}, {}, {}, {}
    for h in range(ML_H):
        t, half = h // 2, h % 2
        own = lo if half == 0 else jnp.logical_not(lo)
        q[h] = jnp.where(own, y[t], 0.0) * ML_DQK ** -0.5
        k[h] = jnp.where(own, y[kbase // LANES + t], 0.0)
        v_aug[h] = jnp.concatenate([v_all[:, h * ML_DV:(h + 1) * ML_DV], ones_col], axis=1)
        b_c = jnp.broadcast_to(bsum[:, ML_H + h:ML_H + h + 1], (tb, LANES))
        ig_c = jnp.broadcast_to(gt[:, h:h + 1], (tb, LANES))
        m_st = m_ref[h]
        for ci in range(nch):
            bc, igc = rows(b_c, ci), rows(ig_c, ci)
            b_r = rows_t[ML_H + h:ML_H + h + 1, ci * c:(ci + 1) * c]
            ig_r = rows_t[h:h + 1, ci * c:(ci + 1) * c]
            b_last = bc[c - 1:c, :]
            d = jnp.where(causal, bc[:, :c] - b_r + ig_r, -jnp.inf)
            m_intra = jnp.max(d, axis=-1, keepdims=True)
            a_end = b_last - bc + igc
            m_end = jnp.max(a_end, axis=0, keepdims=True)
            inter = bc + m_st
            m_t = jnp.maximum(inter, m_intra)
            m_new = jnp.maximum(b_last + m_st, m_end)
            e_d[ci, h] = jnp.exp(d - m_t[:, :c])
            w_inter[ci, h] = jnp.exp(inter - m_t)
            e_mt[ci, h] = jnp.exp(-m_t)
            w_old[ci, h] = jnp.exp(b_last + m_st - m_new)
            kw[ci, h] = rows(k[h], ci) * jnp.exp(a_end - m_new)
            m_st = m_new
        m_ref[h] = m_st

    groups = [list(range(g, min(g + ML_WAVE, nch))) for g in range(0, nch, ML_WAVE)]
    sc, kv = {}, {}

    def intra(grp):
        for ci in grp:
            for h in range(ML_H):
                sc[ci, h] = _mm_nt(rows(q[h], ci), rows(k[h], ci))
                kv[ci, h] = _mm_tn(kw[ci, h], rows(v_aug[h], ci))

    cst = [c_ref[h] for h in range(ML_H)]
    intra(groups[0])
    order = []
    for gi, grp in enumerate(groups):
        order += [("intra", groups[gi + 1])] if gi + 1 < len(groups) else []
        order += [("final", ci) for ci in grp]
    for kind, arg in order:
        if kind == "intra":
            intra(arg)
            continue
        ci = arg
        qc = [_mm(rows(q[h], ci), cst[h]) for h in range(ML_H)]
        pv = [_mm(sc[ci, h] * e_d[ci, h], rows(v_aug[h], ci)) for h in range(ML_H)]
        for h in range(ML_H):
            wi, wo = w_inter[ci, h], w_old[ci, h]
            num = jnp.concatenate([wi, wi], axis=1) * qc[h] + pv[h]
            cst[h] = jnp.concatenate([wo, wo], axis=1) * cst[h] + kv[ci, h]
            den = num[:, ML_DV:ML_DV + 1]
            hv = num[:, :ML_DV] / jnp.maximum(jnp.abs(den), e_mt[ci, h])
            hv = hv * lax.rsqrt(jnp.mean(hv * hv, axis=-1, keepdims=True) + EPS) * ng
            rs, sl = slice(ci * c, (ci + 1) * c), slice(h * ML_DV, (h + 1) * ML_DV)
            og_ref[rs, sl] = (hv * _sigmoid(o_gate[rs, sl]) * _silu(z[rs, sl])).astype(og_ref.dtype)
    for h in range(ML_H):
        c_ref[h] = cst[h]
    o_ref[...] = _residual_ln(xb, og_ref[...], wout_ref[...], lg_ref[...], lb_ref[...])


def _mlstm_layer(x2, bsz, t_len, w_in, conv_w, gate_bias, norm_g, w_out, ln_g, ln_b):
    n, d = x2.shape
    tb = ML_TB
    nb = t_len // tb
    qkw = 2 * ML_H * ML_DQK
    w_in = _pad_cols(w_in.astype(BF16))
    row = lambda b, j: (b * nb + j, 0)
    return pl.pallas_call(
        _mlstm_kernel,
        grid=(bsz, nb),
        in_specs=[
            pl.BlockSpec((tb, d), row),
            _const_spec(w_in.shape),
            _const_spec((CONV_K, qkw)),
            _const_spec((1, LANES)),
            _const_spec((1, LANES)),
            _const_spec(w_out.shape),
            _const_spec((1, d)),
            _const_spec((1, d)),
        ],
        out_specs=pl.BlockSpec((tb, d), row),
        out_shape=jax.ShapeDtypeStruct((n, d), F32),
        scratch_shapes=[pltpu.VMEM((ML_H, LANES, 2 * ML_DV), F32),
                        pltpu.VMEM((ML_H, 1, LANES), F32),
                        pltpu.VMEM((8, qkw), F32),
                        pltpu.VMEM((tb, ML_H * ML_DV), BF16)],
        compiler_params=_cparams(("arbitrary", "arbitrary")),
        name="mlstm_layer",
    )(x2, w_in, conv_w.astype(F32), _lane_vec(gate_bias, 0), norm_g.reshape(1, ML_DV).astype(F32),
      w_out.astype(BF16), ln_g.reshape(1, d).astype(F32), ln_b.reshape(1, d).astype(F32))


def _fox_layer(x2, bsz, t_len, w_in, f_bias, q_norm_g, k_norm_g, w_out, ln_g, ln_b):
    qa, ka, vt, z = _fox_prep(x2, bsz, t_len, w_in, f_bias, q_norm_g, k_norm_g)
    og = _fox_flash(qa, ka, vt, z, bsz, t_len)
    return _out_ln(og, x2, w_out.astype(BF16), ln_g, ln_b)


def kernel(x, a_w_in, a_conv, a_a_log, a_dt_bias, a_norm_g, a_w_out, b_w_in, b_sinks, b_w_out, c_w_in, c_f_bias, c_q_norm, c_k_norm, c_w_out, d_w_in, d_conv, d_gate_bias, d_norm_g, d_w_out, ln_g, ln_b):
    bsz, t_len, d = x.shape
    x2 = x.reshape(bsz * t_len, d)
    for i in range(DEPTH):
        kind, j = i % 4, i // 4
        if kind == 0:
            x2 = _gdn_layer(x2, bsz, t_len, a_w_in[j], a_conv[j], a_a_log[j], a_dt_bias[j],
                            a_norm_g[j], a_w_out[j], ln_g[i], ln_b[i])
        elif kind == 1:
            x2 = _swa_layer(x2, bsz, t_len, b_w_in[j], b_sinks[j], b_w_out[j], ln_g[i], ln_b[i])
        elif kind == 2:
            x2 = _fox_layer(x2, bsz, t_len, c_w_in[j], c_f_bias[j], c_q_norm[j], c_k_norm[j],
                            c_w_out[j], ln_g[i], ln_b[i])
        else:
            x2 = _mlstm_layer(x2, bsz, t_len, d_w_in[j], d_conv[j], d_gate_bias[j], d_norm_g[j],
                              d_w_out[j], ln_g[i], ln_b[i])
    return x2.reshape(bsz, t_len, d)
```

```python
import functools
import math

import jax
import jax.numpy as jnp
from jax import lax
from jax.experimental import pallas as pl
from jax.experimental.pallas import tpu as pltpu

F32 = jnp.float32
BF16 = jnp.bfloat16
LANES = 128
VMEM_LIMIT = 56 * 1024 * 1024

D_MODEL = 1024
DEPTH = 4
CONV_K = 4
ROPE_THETA = 10000.0
EPS = 1e-6
LN_EPS = 1e-5
ALPHA = (2 * DEPTH) ** 0.25
LOG2E = math.log2(math.e)

GDN_H, GDN_DK, GDN_DV = 8, 128, 128
SWA_H, SWA_KVH, SWA_HD, SWA_W = 16, 2, 64, 128
FOX_H, FOX_HD = 16, 64
ML_H, ML_DQK, ML_DV = 8, 64, 128
CHUNK = 64


def _mm(a, b):
    return jnp.dot(a.astype(BF16), b.astype(BF16), preferred_element_type=F32)


def _mm_nt(a, b):
    return lax.dot_general(a.astype(BF16), b.astype(BF16), (((1,), (1,)), ((), ())),
                           preferred_element_type=F32)


def _mm_tn(a, b):
    return lax.dot_general(a.astype(BF16), b.astype(BF16), (((0,), (0,)), ((), ())),
                           preferred_element_type=F32)


def _mm_f32(a, b):
    return jnp.dot(a, b, preferred_element_type=F32, precision=lax.Precision.HIGHEST)


def _mm_nt_f32(a, b):
    return lax.dot_general(a, b, (((1,), (1,)), ((), ())), preferred_element_type=F32,
                           precision=lax.Precision.HIGHEST)


def _sigmoid(x):
    return 1.0 / (1.0 + jnp.exp(-x))


def _silu(x):
    return x * _sigmoid(x)


def _softplus(x):
    return jnp.maximum(x, 0.0) + jnp.log1p(jnp.exp(-jnp.abs(x)))


def _log_sigmoid(x):
    return -_softplus(-x)


def _iota(shape, dim):
    return lax.broadcasted_iota(jnp.int32, shape, dim)


def _tri(n):
    return (_iota((n, n), 0) >= _iota((n, n), 1)).astype(F32)


def _eye(n, m):
    return (_iota((n, m), 0) == _iota((n, m), 1)).astype(F32)


def _conv_silu(x, halo, w):
    n = x.shape[0]
    xx = jnp.concatenate([halo, x], axis=0)
    y = w[CONV_K - 1:CONV_K] * x
    for j in range(1, CONV_K):
        y = y + w[CONV_K - 1 - j:CONV_K - j] * pltpu.roll(xx, j, 0)[8:8 + n]
    return _silu(y)


def _cparams(sem):
    return pltpu.CompilerParams(dimension_semantics=sem, vmem_limit_bytes=VMEM_LIMIT)


def _pad_cols(w, mult=LANES):
    pad = (-w.shape[1]) % mult
    return jnp.pad(w, ((0, 0), (0, pad))) if pad else w


def _lane_vec(vals, offset):
    return jnp.zeros((1, LANES), F32).at[0, offset:offset + vals.shape[0]].set(vals.astype(F32))


def _const_spec(shape):
    return pl.BlockSpec(shape, lambda *_: (0,) * len(shape), pipeline_mode=pl.Buffered(1))


def _residual_ln(xb, og, w_out, g, b):
    y = ALPHA * xb + _mm(og, w_out)
    mu = jnp.mean(y, axis=-1, keepdims=True)
    yc = y - mu
    var = jnp.mean(yc * yc, axis=-1, keepdims=True)
    return yc * lax.rsqrt(var + LN_EPS) * g + b


def _out_ln_kernel(og_ref, x_ref, w_ref, g_ref, b_ref, o_ref):
    o_ref[...] = _residual_ln(x_ref[...], og_ref[...], w_ref[...], g_ref[...], b_ref[...])


def _out_ln(og, x2, w, g, b):
    n, d = x2.shape
    wd = og.shape[1]
    tm = 512
    return pl.pallas_call(
        _out_ln_kernel,
        grid=(n // tm,),
        in_specs=[pl.BlockSpec((tm, wd), lambda i: (i, 0)),
                  pl.BlockSpec((tm, d), lambda i: (i, 0)),
                  pl.BlockSpec((wd, d), lambda i: (0, 0)),
                  pl.BlockSpec((1, d), lambda i: (0, 0)),
                  pl.BlockSpec((1, d), lambda i: (0, 0))],
        out_specs=pl.BlockSpec((tm, d), lambda i: (i, 0)),
        out_shape=jax.ShapeDtypeStruct((n, d), F32),
        compiler_params=_cparams(("parallel",)),
        name="out_proj_ln",
    )(og, x2, w, g.reshape(1, d).astype(F32), b.reshape(1, d).astype(F32))


GDN_TB = 256
GDN_WAVE = 2


def _gdn_kernel(x_ref, win_ref, cw_ref, alog_ref, dtb_ref, ng_ref, wout_ref, lg_ref, lb_ref,
                o_ref, s_ref, halo_ref, og_ref):
    j = pl.program_id(1)
    c, tb = CHUNK, GDN_TB
    nch = tb // c
    probs = [(ci, h) for ci in range(nch) for h in range(GDN_H)]
    cw = 3 * GDN_H * 128

    @pl.when(j == 0)
    def _():
        s_ref[...] = jnp.zeros_like(s_ref)
        halo_ref[...] = jnp.zeros_like(halo_ref)

    xb = x_ref[...]
    p = _mm(xb, win_ref[...])

    def conv_tile(col):
        sl = slice(col, col + LANES)
        return _conv_silu(p[:, sl], halo_ref[:, sl], cw_ref[:, sl])

    z = p[:, cw:cw + GDN_H * GDN_DV]

    gat = p[:, cw + GDN_H * GDN_DV:cw + GDN_H * GDN_DV + LANES]
    beta = _sigmoid(gat)
    g = -jnp.exp(alog_ref[...]) * _softplus(gat + dtb_ref[...])
    rt, ct = _iota((tb, tb), 0), _iota((tb, tb), 1)
    tri_bd = ((rt >= ct) & (rt // c == ct // c)).astype(F32)
    gc = _mm_f32(tri_bd, g)
    gct = _mm_nt_f32(_eye(LANES, LANES), gc)

    row = _iota((c, c), 0)
    col = _iota((c, c), 1)
    causal = row >= col
    strict = row > col
    ng = ng_ref[...]
    qoff, koff, voff = 0, GDN_H * GDN_DK, 2 * GDN_H * GDN_DK

    def rows(a, ci):
        return a[ci * c:(ci + 1) * c]

    kn, kb, qn, qd, kd, rhs, gl = {}, {}, {}, {}, {}, {}, {}
    neg_l, qk, m = {}, {}, {}
    for h in range(GDN_H):
        q = conv_tile(qoff + h * 128)
        k = conv_tile(koff + h * 128)
        v = conv_tile(voff + h * 128)
        q = q * (lax.rsqrt(jnp.sum(q * q, axis=-1, keepdims=True) + EPS) * GDN_DK ** -0.5)
        k = k * lax.rsqrt(jnp.sum(k * k, axis=-1, keepdims=True) + EPS)
        beta_c = beta[:, h:h + 1]
        gc_c = jnp.broadcast_to(gc[:, 8 + h:9 + h], (tb, LANES))
        egc = jnp.exp(gc_c)
        g_last = jnp.concatenate(
            [jnp.broadcast_to(gc_c[(ci + 1) * c - 1:(ci + 1) * c, :], (c, LANES))
             for ci in range(nch)], axis=0)
        kbh = k * beta_c
        kn[h], kb[h], qn[h] = k, kbh, q
        qd[h] = q * egc
        kd[h] = k * jnp.exp(g_last - gc_c)
        rhs[h] = jnp.concatenate([v * beta_c, kbh * egc], axis=1)
        gl[h] = g_last
    halo_ref[...] = p[tb - 8:tb, :cw]

    uw, qk = {}, {}

    def wy_stages(keys):
        neg_l = {}
        for ci, h in keys:
            gmat = _mm_nt(jnp.concatenate([rows(kb[h], ci), rows(qn[h], ci)], axis=0), rows(kn[h], ci))
            gc_c = jnp.broadcast_to(gc[ci * c:(ci + 1) * c, 8 + h:9 + h], (c, c))
            gc_r = gct[8 + h:9 + h, ci * c:(ci + 1) * c]
            decay = jnp.exp(jnp.where(causal, gc_c - gc_r, -jnp.inf))
            neg_l[ci, h] = jnp.where(strict, -(gmat[:c] * decay), 0.0)
            qk[ci, h] = jnp.where(causal, gmat[c:] * decay, 0.0)
        yield
        pw = dict(neg_l)
        m = {key: _mm(neg_l[key], neg_l[key]) for key in keys}
        yield
        nsq = int(math.log2(c)) - 1
        for it in range(nsq):
            if it < nsq - 1:
                r = {key: _mm(jnp.concatenate([pw[key], m[key]], axis=0), m[key]) for key in keys}
                pw = {key: pw[key] + m[key] + r[key][:c] for key in keys}
                m = {key: r[key][c:] for key in keys}
            else:
                pw = {key: pw[key] + m[key] + _mm(pw[key], m[key]) for key in keys}
            yield
        for ci, h in keys:
            rh = rows(rhs[h], ci)
            uw[ci, h] = rh + _mm(pw[ci, h], rh)
        yield

    state = {"s": [s_ref[h] for h in range(GDN_H)]}

    def recurrence(chunks):
        for ci in chunks:
            s = state["s"]
            t = [_mm(jnp.concatenate([uw[ci, h][:, 128:], rows(qd[h], ci)], axis=0), s[h])
                 for h in range(GDN_H)]
            yield
            v_new = [uw[ci, h][:, :128] - t[h][:c] for h in range(GDN_H)]
            o = [t[h][c:] + _mm(qk[ci, h], v_new[h]) for h in range(GDN_H)]
            state["s"] = [s[h] * jnp.exp(gl[h][ci * c:ci * c + 1, :])
                          + _mm_tn(rows(kd[h], ci), v_new[h]) for h in range(GDN_H)]
            yield
            for h in range(GDN_H):
                oh = o[h] * lax.rsqrt(jnp.mean(o[h] * o[h], axis=-1, keepdims=True) + EPS) * ng
                zt = z[ci * c:(ci + 1) * c, h * 128:(h + 1) * 128]
                og_ref[ci * c:(ci + 1) * c, h * 128:(h + 1) * 128] = (oh * _silu(zt)).astype(og_ref.dtype)

    groups = [list(range(g, min(g + GDN_WAVE, nch))) for g in range(0, nch, GDN_WAVE)]
    prev = None
    for grp in groups + [None]:
        live = []
        if grp is not None:
            live.append(wy_stages([(ci, h) for ci in grp for h in range(GDN_H)]))
        if prev is not None:
            live.append(recurrence(prev))
        while live:
            for g in list(live):
                if next(g, "done") == "done":
                    live.remove(g)
        prev = grp
    s = state["s"]
    for h in range(GDN_H):
        s_ref[h] = s[h]
    o_ref[...] = _residual_ln(xb, og_ref[...], wout_ref[...], lg_ref[...], lb_ref[...])


def _gdn_layer(x2, bsz, t_len, w_in, conv_w, a_log, dt_bias, norm_g, w_out, ln_g, ln_b):
    n, d = x2.shape
    tb = GDN_TB
    nb = t_len // tb
    cw = 3 * GDN_H * 128
    w_in = _pad_cols(w_in.astype(BF16))
    row = lambda b, j: (b * nb + j, 0)
    return pl.pallas_call(
        _gdn_kernel,
        grid=(bsz, nb),
        in_specs=[
            pl.BlockSpec((tb, d), row),
            _const_spec(w_in.shape),
            _const_spec((CONV_K, cw)),
            _const_spec((1, LANES)),
            _const_spec((1, LANES)),
            _const_spec((1, LANES)),
            _const_spec(w_out.shape),
            _const_spec((1, d)),
            _const_spec((1, d)),
        ],
        out_specs=pl.BlockSpec((tb, d), row),
        out_shape=jax.ShapeDtypeStruct((n, d), F32),
        scratch_shapes=[pltpu.VMEM((GDN_H, GDN_DK, GDN_DV), F32),
                        pltpu.VMEM((8, cw), F32),
                        pltpu.VMEM((tb, GDN_H * GDN_DV), BF16)],
        compiler_params=_cparams(("arbitrary", "arbitrary")),
        name="gdn_layer",
    )(x2, w_in, conv_w.astype(F32), _lane_vec(a_log, 8), _lane_vec(dt_bias, 8),
      norm_g.reshape(1, GDN_DV).astype(F32), w_out.astype(BF16),
      ln_g.reshape(1, d).astype(F32), ln_b.reshape(1, d).astype(F32))


def _rope_tile(x, cos, sin_signed, first_half):
    rot = jnp.where(first_half, pltpu.roll(x, LANES - SWA_HD // 2, 1), pltpu.roll(x, SWA_HD // 2, 1))
    return x * cos + rot * sin_signed


SWA_TB = 256


def _swa_kernel(sink_ref, x_ref, win_ref, cos_ref, sin_ref, wout_ref, lg_ref, lb_ref,
                o_ref, kp_ref, vp_ref, og_ref):
    j = pl.program_id(1)
    w, tb = SWA_W, SWA_TB
    nw = tb // w
    q_w, kv_w = SWA_H * SWA_HD, SWA_KVH * SWA_HD

    @pl.when(j == 0)
    def _():
        kp_ref[...] = jnp.zeros_like(kp_ref)
        vp_ref[...] = jnp.zeros_like(vp_ref)

    xb = x_ref[...]
    p = _mm(xb, win_ref[...])
    z = p[:, q_w + 2 * kv_w:]
    cos = cos_ref[...]
    sin = sin_ref[...]
    lane = _iota((w, LANES), 1)
    lo = lane < SWA_HD
    lane_t = _iota((tb, LANES), 1)
    first_half = (lane_t % SWA_HD) < SWA_HD // 2
    lo2 = _iota((2 * w, LANES), 1) < SWA_HD

    kr = _rope_tile(p[:, q_w:q_w + kv_w], cos, sin, first_half)
    vc = p[:, q_w + kv_w:q_w + 2 * kv_w]
    kall = jnp.concatenate([kp_ref[...], kr], axis=0)
    vall = jnp.concatenate([vp_ref[...], vc], axis=0)
    kp_ref[...] = kr[tb - w:]
    vp_ref[...] = vc[tb - w:]

    r = _iota((2 * w, 2 * w), 0) % w
    cidx = _iota((2 * w, 2 * w), 1)
    in_prev = (cidx < w) & (cidx > r)
    in_cur = (cidx >= w) & (cidx - w <= r)
    top = _iota((2 * w, 1), 0) < w

    group = SWA_H // SWA_KVH
    tiles = [(wi, t) for wi in range(nw) for t in range(SWA_H // 2)]
    kdup, vdup = {}, {}
    for wi in range(nw):
        kcat = kall[wi * w:(wi + 2) * w]
        vcat = vall[wi * w:(wi + 2) * w]
        kswap = pltpu.roll(kcat, SWA_HD, 1)
        vswap = pltpu.roll(vcat, SWA_HD, 1)
        for kvh in range(SWA_KVH):
            own = lo2 if kvh == 0 else jnp.logical_not(lo2)
            kdup[wi, kvh] = jnp.where(own, kcat, kswap)
            vdup[wi, kvh] = jnp.where(own, vcat, vswap)
    qr = {}
    for t in range(SWA_H // 2):
        qr[t] = _rope_tile(p[:, t * LANES:(t + 1) * LANES], cos, sin, first_half) * SWA_HD ** -0.5
    scores = {}
    for wi, t in tiles:
        qt = qr[t][wi * w:(wi + 1) * w]
        q2 = jnp.concatenate([jnp.where(lo, qt, 0.0), jnp.where(lo, 0.0, qt)], axis=0)
        valid = ((in_prev & (j > 0)) | in_cur) if wi == 0 else (in_prev | in_cur)
        scores[wi, t] = jnp.where(valid, _mm_nt(q2, kdup[wi, t // (group // 2)]), -jnp.inf)
    probs = {}
    for wi, t in tiles:
        s = scores[wi, t]
        sk = jnp.where(top, sink_ref[2 * t], sink_ref[2 * t + 1])
        m = jnp.maximum(jnp.max(s, axis=-1, keepdims=True), sk)
        e = jnp.exp(s - m)
        den = jnp.sum(e, axis=-1, keepdims=True) + jnp.exp(sk - m)
        probs[wi, t] = e / den
    for wi, t in tiles:
        o2 = _mm(probs[wi, t], vdup[wi, t // (group // 2)])
        ot = jnp.where(lo, o2[:w], o2[w:])
        zt = z[wi * w:(wi + 1) * w, t * LANES:(t + 1) * LANES]
        og_ref[wi * w:(wi + 1) * w, t * LANES:(t + 1) * LANES] = (ot * _silu(zt)).astype(og_ref.dtype)
    o_ref[...] = _residual_ln(xb, og_ref[...], wout_ref[...], lg_ref[...], lb_ref[...])


def _swa_layer(x2, bsz, t_len, w_in, sinks, w_out, ln_g, ln_b):
    n, d = x2.shape
    tb = SWA_TB
    nb = t_len // tb
    q_w = SWA_H * SWA_HD
    cos_t, sin_t = _rope_tables(t_len)
    row = lambda b, j: (b * nb + j, 0)
    return pl.pallas_call(
        _swa_kernel,
        grid=(bsz, nb),
        in_specs=[
            pl.BlockSpec(memory_space=pltpu.SMEM),
            pl.BlockSpec((tb, d), row),
            _const_spec(w_in.shape),
            pl.BlockSpec((tb, LANES), lambda b, j: (j, 0)),
            pl.BlockSpec((tb, LANES), lambda b, j: (j, 0)),
            _const_spec(w_out.shape),
            _const_spec((1, d)),
            _const_spec((1, d)),
        ],
        out_specs=pl.BlockSpec((tb, d), row),
        out_shape=jax.ShapeDtypeStruct((n, d), F32),
        scratch_shapes=[pltpu.VMEM((SWA_W, LANES), F32), pltpu.VMEM((SWA_W, LANES), F32),
                        pltpu.VMEM((tb, q_w), BF16)],
        compiler_params=_cparams(("arbitrary", "arbitrary")),
        name="swa_layer",
    )(sinks.astype(F32), x2, w_in.astype(BF16), cos_t, sin_t, w_out.astype(BF16),
      ln_g.reshape(1, d).astype(F32), ln_b.reshape(1, d).astype(F32))


def _rope_tables(t_len):
    half = SWA_HD // 2
    inv = ROPE_THETA ** (-jnp.arange(half, dtype=F32) / half)
    ang = jnp.arange(t_len, dtype=F32)[:, None] * inv[None, :]
    cos, sin = jnp.cos(ang), jnp.sin(ang)
    return (jnp.concatenate([cos, cos, cos, cos], axis=1),
            jnp.concatenate([-sin, sin, -sin, sin], axis=1))


FOX_BIAS_LANES = 3


def _split3(x):
    x1 = x.astype(BF16).astype(F32)
    r1 = x - x1
    x2 = r1.astype(BF16).astype(F32)
    x3 = (r1 - x2).astype(BF16).astype(F32)
    return x1, x2, x3


FOX_TB = 256
FOX_VR = FOX_HD + 16


def _fox_prep_kernel(x_ref, win_ref, fb_ref, qg_ref, kg_ref,
                     qa_ref, ka_ref, vt_ref, z_ref, carry_ref):
    j = pl.program_id(1)
    tb = FOX_TB
    wd = FOX_H * FOX_HD

    @pl.when(j == 0)
    def _():
        carry_ref[...] = jnp.zeros_like(carry_ref)

    p = _mm(x_ref[...], win_ref[...])
    z_ref[...] = p[:, 3 * wd:4 * wd]
    logf = _log_sigmoid(p[:, 4 * wd:4 * wd + LANES] + fb_ref[...])
    cum = _mm_f32(_tri(tb), logf) + carry_ref[...]
    carry_ref[...] = cum[tb - 1:tb, :]
    cum2 = cum * LOG2E

    lane = _iota((tb, LANES), 1)
    ones2 = ((_iota((LANES, LANES), 0) // FOX_HD) == (_iota((LANES, LANES), 1) // FOX_HD)).astype(BF16)

    def norm(x, g):
        x2 = x * x
        hi = x2.astype(BF16)
        lo = (x2 - hi.astype(F32)).astype(BF16)
        ss = (jnp.dot(hi, ones2, preferred_element_type=F32)
              + jnp.dot(lo, ones2, preferred_element_type=F32))
        return x * lax.rsqrt(ss * (1.0 / FOX_HD) + EPS) * g

    head_lane = lane < FOX_HD
    bl = lane - FOX_HD
    ones_rows = (_iota((FOX_VR - FOX_HD, tb), 0) == 0).astype(BF16)
    cum_split = _split3(cum2)
    ones_q = ((bl >= 3) & (bl < 6)).astype(F32)
    ones_k = ((bl >= 0) & (bl < 3)).astype(F32)
    for t in range(FOX_H // 2):
        sl = slice(t * LANES, (t + 1) * LANES)
        qn = norm(p[:, t * LANES:(t + 1) * LANES], qg_ref[...]) * (FOX_HD ** -0.5 * LOG2E)
        kn = norm(p[:, wd + t * LANES:wd + (t + 1) * LANES], kg_ref[...])
        vt2 = p[:, 2 * wd + t * LANES:2 * wd + (t + 1) * LANES].T.astype(BF16)
        for half in range(2):
            h = 2 * t + half
            c1, c2, c3 = (jnp.broadcast_to(cs[:, h:h + 1], (tb, LANES)) for cs in cum_split)
            qb = jnp.where(bl == 0, c1, jnp.where(bl == 1, c2, jnp.where(bl == 2, c3, ones_q)))
            kb = jnp.where(bl == 3, -c1, jnp.where(bl == 4, -c2, jnp.where(bl == 5, -c3, ones_k)))
            qh = qn if half == 0 else pltpu.roll(qn, FOX_HD, 1)
            kh = kn if half == 0 else pltpu.roll(kn, FOX_HD, 1)
            qa_ref[:, h * LANES:(h + 1) * LANES] = jnp.where(head_lane, qh, qb).astype(BF16)
            ka_ref[:, h * LANES:(h + 1) * LANES] = jnp.where(head_lane, kh, kb).astype(BF16)
            vt_ref[h * FOX_VR:h * FOX_VR + FOX_HD, :] = vt2[half * FOX_HD:(half + 1) * FOX_HD]
            vt_ref[h * FOX_VR + FOX_HD:(h + 1) * FOX_VR, :] = ones_rows


def _fox_prep(x2, bsz, t_len, w_in, f_bias, q_norm_g, k_norm_g):
    n, d = x2.shape
    tb = FOX_TB
    nb = t_len // tb
    wd = FOX_H * FOX_HD
    w_in = _pad_cols(w_in.astype(BF16))
    row = lambda b, j: (b * nb + j, 0)
    g2 =lambda g: jnp.concatenate([g, g]).reshape(1, LANES).astype(F32)
    return pl.pallas_call(
        _fox_prep_kernel,
        grid=(bsz, nb),
        in_specs=[
            pl.BlockSpec((tb, d), row),
            _const_spec(w_in.shape),
            _const_spec((1, LANES)),
            _const_spec((1, LANES)),
            _const_spec((1, LANES)),
        ],
        out_specs=[
            pl.BlockSpec((tb, 2 * wd), row),
            pl.BlockSpec((tb, 2 * wd), row),
            pl.BlockSpec((FOX_H * FOX_VR, tb), lambda b, j: (b, j)),
            pl.BlockSpec((tb, wd), row),
        ],
        out_shape=[jax.ShapeDtypeStruct((n, 2 * wd), BF16),
                   jax.ShapeDtypeStruct((n, 2 * wd), BF16),
                   jax.ShapeDtypeStruct((bsz * FOX_H * FOX_VR, t_len), BF16),
                   jax.ShapeDtypeStruct((n, wd), F32)],
        scratch_shapes=[pltpu.VMEM((1, LANES), F32)],
        compiler_params=_cparams(("arbitrary", "arbitrary")),
        name="fox_prep",
    )(x2, w_in, _lane_vec(f_bias, 0), g2(q_norm_g), g2(k_norm_g))


FOX_TQ = 512
FOX_HPS = 4
FOX_KC = 16


def _fox_flash_kernel(qa_ref, ka_ref, vt_ref, z_ref, o_ref, s_scr, e_scr, m_scr, acc_scr):
    i = pl.program_id(2)
    tq, nh = FOX_TQ, FOX_HPS
    m_scr[...] = jnp.full(m_scr.shape, -jnp.inf, F32)
    acc_scr[...] = jnp.zeros_like(acc_scr)
    q = [qa_ref[:, hh * LANES:(hh + 1) * LANES] for hh in range(nh)]
    causal_c = _iota((FOX_KC, tq), 0) - _iota((FOX_KC, tq), 1)

    def qk_head(jb, slot, hh):
        start = pl.multiple_of(jb * tq, tq)
        s_scr[slot, hh] = lax.dot_general(
            ka_ref[pl.ds(start, tq), hh * LANES:(hh + 1) * LANES], q[hh],
            (((1,), (1,)), ((), ())), preferred_element_type=F32)

    def qk(jb, slot):
        for hh in range(nh):
            qk_head(jb, slot, hh)

    def softmax_pv(jb, slot, masked, ahead=None):
        start = pl.multiple_of(jb * tq, tq)

        def scores(hh, c):
            s = s_scr[slot, hh, c * FOX_KC:(c + 1) * FOX_KC, :]
            return jnp.where(causal_c <= -c * FOX_KC, s, -jnp.inf) if masked else s

        nkc = tq // FOX_KC
        for hh in range(nh):
            if ahead is not None:
                qk_head(ahead[0], ahead[1], hh)
            m_prev = m_scr[hh]
            mx = functools.reduce(jnp.maximum, [scores(hh, c) for c in range(nkc)])
            m_new = jnp.maximum(m_prev, jnp.max(mx, axis=0, keepdims=True))
            a = jnp.exp2(m_prev - m_new)
            for c in range(nkc):
                e_scr[hh, c * FOX_KC:(c + 1) * FOX_KC, :] = jnp.exp2(scores(hh, c) - m_new).astype(BF16)
            m_scr[hh] = m_new
            vt = vt_ref[hh * FOX_VR:(hh + 1) * FOX_VR, pl.ds(start, tq)]
            acc_scr[hh] = a * acc_scr[hh] + jnp.dot(vt, e_scr[hh], preferred_element_type=F32)

    qk(0, 0)

    def body(t, carry):
        softmax_pv(2 * t, 0, False, ahead=(2 * t + 1, 1))
        softmax_pv(2 * t + 1, 1, False, ahead=(2 * t + 2, 0))
        return carry

    lax.fori_loop(0, i // 2, body, 0)

    @pl.when(i % 2 == 0)
    def _():
        softmax_pv(i, 0, True)

    @pl.when(i % 2 == 1)
    def _():
        softmax_pv(i - 1, 0, False, ahead=(i, 1))
        softmax_pv(i, 1, True)

    for t in range(nh // 2):
        ot = jnp.concatenate([acc_scr[hh, :FOX_HD] / acc_scr[hh, FOX_HD:FOX_HD + 1]
                              for hh in (2 * t, 2 * t + 1)], axis=0)
        sl = slice(t * LANES, (t + 1) * LANES)
        o_ref[:, sl] = (ot.T * _silu(z_ref[:, sl])).astype(o_ref.dtype)


def _fox_flash(qa, ka, vt, z, bsz, t_len):
    n = qa.shape[0]
    tq = FOX_TQ
    nq = t_len // tq
    wd = FOX_H * FOX_HD
    nh = FOX_HPS
    ng = FOX_H // nh
    return pl.pallas_call(
        _fox_flash_kernel,
        grid=(bsz, ng, nq),
        in_specs=[
            pl.BlockSpec((tq, nh * LANES), lambda b, hg, i: (b * nq + i, hg)),
            pl.BlockSpec((t_len, nh * LANES), lambda b, hg, i: (b, hg)),
            pl.BlockSpec((nh * FOX_VR, t_len), lambda b, hg, i: (b * ng + hg, 0)),
            pl.BlockSpec((tq, nh // 2 * LANES), lambda b, hg, i: (b * nq + i, hg)),
        ],
        out_specs=pl.BlockSpec((tq, nh // 2 * LANES), lambda b, hg, i: (b * nq + i, hg)),
        out_shape=jax.ShapeDtypeStruct((n, wd), BF16),
        scratch_shapes=[pltpu.VMEM((2, nh, tq, tq), F32), pltpu.VMEM((nh, tq, tq), BF16),
                        pltpu.VMEM((nh, 1, tq), F32), pltpu.VMEM((nh, FOX_VR, tq), F32)],
        compiler_params=_cparams(("parallel", "parallel", "arbitrary")),
        name="fox_flash",
    )(qa, ka, vt, z)


ML_TB = 256
ML_WAVE = 2


def _mlstm_kernel(x_ref, win_ref, cw_ref, gb_ref, ng_ref, wout_ref, lg_ref, lb_ref,
                  o_ref, c_ref, m_ref, halo_ref, og_ref):
    j = pl.program_id(1)
    c, tb = CHUNK, ML_TB
    nch = tb // c
    probs = [(ci, h) for ci in range(nch) for h in range(ML_H)]
    qkw, vw = 2 * ML_H * ML_DQK, ML_H * ML_DV

    @pl.when(j == 0)
    def _():
        c_ref[...] = jnp.zeros_like(c_ref)
        m_ref[...] = jnp.zeros_like(m_ref)
        halo_ref[...] = jnp.zeros_like(halo_ref)

    xb = x_ref[...]
    p = _mm(xb, win_ref[...])
    y = [_conv_silu(p[:, t * LANES:(t + 1) * LANES], halo_ref[:, t * LANES:(t + 1) * LANES],
                    cw_ref[:, t * LANES:(t + 1) * LANES]) for t in range(qkw // LANES)]
    halo_ref[...] = p[tb - 8:tb, :qkw]
    v_all = p[:, qkw:qkw + vw]
    o_gate = p[:, qkw + vw:qkw + 2 * vw]
    z = p[:, qkw + 2 * vw:qkw + 3 * vw]

    gt = p[:, qkw + 3 * vw:qkw + 3 * vw + LANES] + gb_ref[...]
    lane = _iota((tb, LANES), 1)
    rt, ct = _iota((tb, tb), 0), _iota((tb, tb), 1)
    tri_bd = ((rt >= ct) & (rt // c == ct // c)).astype(F32)
    bsum = _mm_f32(tri_bd, _log_sigmoid(gt))
    rows_t = _mm_nt_f32(_eye(LANES, LANES), jnp.where(lane < ML_H, gt, bsum))

    causal = _iota((c, c), 0) >= _iota((c, c), 1)
    lo = lane < ML_DQK
    ones_col = (lane == 0).astype(F32)
    ng = ng_ref[...]
    kbase = ML_H * ML_DQK

    def rows(a, ci):
        return a[ci * c:(ci + 1) * c]

    q, k, v_aug = {}, {}, {}
    e_d, w_inter, e_mt, kw, w_old = {}, {}, {}, {}, {}
    for h in range(ML_H):
        t, half = h // 2, h % 2
        own = lo if half == 0 else jnp.logical_not(lo)
        q[h] = jnp.where(own, y[t], 0.0) * ML_DQK ** -0.5
        k[h] = jnp.where(own, y[kbase // LANES + t], 0.0)
        v_aug[h] = jnp.concatenate([v_all[:, h * ML_DV:(h + 1) * ML_DV], ones_col], axis=1)
        b_c = jnp.broadcast_to(bsum[:, ML_H + h:ML_H + h + 1], (tb, LANES))
        ig_c = jnp.broadcast_to(gt[:, h:h + 1], (tb, LANES))
        m_st = m_ref[h]
        for ci in range(nch):
            bc, igc = rows(b_c, ci), rows(ig_c, ci)
            b_r = rows_t[ML_H + h:ML_H + h + 1, ci * c:(ci + 1) * c]
            ig_r = rows_t[h:h + 1, ci * c:(ci + 1) * c]
            b_last = bc[c - 1:c, :]
            d = jnp.where(causal, bc[:, :c] - b_r + ig_r, -jnp.inf)
            m_intra = jnp.max(d, axis=-1, keepdims=True)
            a_end = b_last - bc + igc
            m_end = jnp.max(a_end, axis=0, keepdims=True)
            inter = bc + m_st
            m_t = jnp.maximum(inter, m_intra)
            m_new = jnp.maximum(b_last + m_st, m_end)
            e_d[ci, h] = jnp.exp(d - m_t[:, :c])
            w_inter[ci, h] = jnp.exp(inter - m_t)
            e_mt[ci, h] = jnp.exp(-m_t)
            w_old[ci, h] = jnp.exp(b_last + m_st - m_new)
            kw[ci, h] = rows(k[h], ci) * jnp.exp(a_end - m_new)
            m_st = m_new
        m_ref[h] = m_st

    groups = [list(range(g, min(g + ML_WAVE, nch))) for g in range(0, nch, ML_WAVE)]
    sc, kv = {}, {}

    def intra(grp):
        for ci in grp:
            for h in range(ML_H):
                sc[ci, h] = _mm_nt(rows(q[h], ci), rows(k[h], ci))
                kv[ci, h] = _mm_tn(kw[ci, h], rows(v_aug[h], ci))

    cst = [c_ref[h] for h in range(ML_H)]
    intra(groups[0])
    order = []
    for gi, grp in enumerate(groups):
        order += [("intra", groups[gi + 1])] if gi + 1 < len(groups) else []
        order += [("final", ci) for ci in grp]
    for kind, arg in order:
        if kind == "intra":
            intra(arg)
            continue
        ci = arg
        qc = [_mm(rows(q[h], ci), cst[h]) for h in range(ML_H)]
        pv = [_mm(sc[ci, h] * e_d[ci, h], rows(v_aug[h], ci)) for h in range(ML_H)]
        for h in range(ML_H):
            wi, wo = w_inter[ci, h], w_old[ci, h]
            num = jnp.concatenate([wi, wi], axis=1) * qc[h] + pv[h]
            cst[h] = jnp.concatenate([wo, wo], axis=1) * cst[h] + kv[ci, h]
            den = num[:, ML_DV:ML_DV + 1]
            hv = num[:, :ML_DV] / jnp.maximum(jnp.abs(den), e_mt[ci, h])
            hv = hv * lax.rsqrt(jnp.mean(hv * hv, axis=-1, keepdims=True) + EPS) * ng
            rs, sl = slice(ci * c, (ci + 1) * c), slice(h * ML_DV, (h + 1) * ML_DV)
            og_ref[rs, sl] = (hv * _sigmoid(o_gate[rs, sl]) * _silu(z[rs, sl])).astype(og_ref.dtype)
    for h in range(ML_H):
        c_ref[h] = cst[h]
    o_ref[...] = _residual_ln(xb, og_ref[...], wout_ref[...], lg_ref[...], lb_ref[...])


def _mlstm_layer(x2, bsz, t_len, w_in, conv_w, gate_bias, norm_g, w_out, ln_g, ln_b):
    n, d = x2.shape
    tb = ML_TB
    nb = t_len // tb
    qkw = 2 * ML_H * ML_DQK
    w_in = _pad_cols(w_in.astype(BF16))
    row = lambda b, j: (b * nb + j, 0)
    return pl.pallas_call(
        _mlstm_kernel,
        grid=(bsz, nb),
        in_specs=[
            pl.BlockSpec((tb, d), row),
            _const_spec(w_in.shape),
            _const_spec((CONV_K, qkw)),
            _const_spec((1, LANES)),
            _const_spec((1, LANES)),
            _const_spec(w_out.shape),
            _const_spec((1, d)),
            _const_spec((1, d)),
        ],
        out_specs=pl.BlockSpec((tb, d), row),
        out_shape=jax.ShapeDtypeStruct((n, d), F32),
        scratch_shapes=[pltpu.VMEM((ML_H, LANES, 2 * ML_DV), F32),
                        pltpu.VMEM((ML_H, 1, LANES), F32),
                        pltpu.VMEM((8, qkw), F32),
                        pltpu.VMEM((tb, ML_H * ML_DV), BF16)],
        compiler_params=_cparams(("arbitrary", "arbitrary")),
        name="mlstm_layer",
    )(x2, w_in, conv_w.astype(F32), _lane_vec(gate_bias, 0), norm_g.reshape(1, ML_DV).astype(F32),
      w_out.astype(BF16), ln_g.reshape(1, d).astype(F32), ln_b.reshape(1, d).astype(F32))


def _fox_layer(x2, bsz, t_len, w_in, f_bias, q_norm_g, k_norm_g, w_out, ln_g, ln_b):
    qa, ka, vt, z = _fox_prep(x2, bsz, t_len, w_in, f_bias, q_norm_g, k_norm_g)
    og = _fox_flash(qa, ka, vt, z, bsz, t_len)
    return _out_ln(og, x2, w_out.astype(BF16), ln_g, ln_b)


def kernel(x, a_w_in, a_conv, a_a_log, a_dt_bias, a_norm_g, a_w_out, b_w_in, b_sinks, b_w_out, c_w_in, c_f_bias, c_q_norm, c_k_norm, c_w_out, d_w_in, d_conv, d_gate_bias, d_norm_g, d_w_out, ln_g, ln_b):
    bsz, t_len, d = x.shape
    x2 = x.reshape(bsz * t_len, d)
    for i in range(DEPTH):
        kind, j = i % 4, i // 4
        if kind == 0:
            x2 = _gdn_layer(x2, bsz, t_len, a_w_in[j], a_conv[j], a_a_log[j], a_dt_bias[j],
                            a_norm_g[j], a_w_out[j], ln_g[i], ln_b[i])
        elif kind == 1:
            x2 = _swa_layer(x2, bsz, t_len, b_w_in[j], b_sinks[j], b_w_out[j], ln_g[i], ln_b[i])
        elif kind == 2:
            x2 = _fox_layer(x2, bsz, t_len, c_w_in[j], c_f_bias[j], c_q_norm[j], c_k_norm[j],
                            c_w_out[j], ln_g[i], ln_b[i])
        else:
            x2 = _mlstm_layer(x2, bsz, t_len, d_w_in[j], d_conv[j], d_gate_bias[j], d_norm_g[j],
                              d_w_out[j], ln_g[i], ln_b[i])
    return x2.reshape(bsz, t_len, d)
```

```python
import functools
import math

import jax
import jax.numpy as jnp
from jax import lax
from jax.experimental import pallas as pl
from jax.experimental.pallas import tpu as pltpu

F32 = jnp.float32
BF16 = jnp.bfloat16
LANES = 128
VMEM_LIMIT = 56 * 1024 * 1024

DEPTH = 4
CONV_K = 4
ROPE_THETA = 10000.0
EPS = 1e-6
LN_EPS = 1e-5
ALPHA = (2 * DEPTH) ** 0.25
LOG2E = math.log2(math.e)

GDN_H, GDN_DK, GDN_DV = 8, 128, 128
SWA_H, SWA_KVH, SWA_HD, SWA_W = 16, 2, 64, 128
FOX_H, FOX_HD = 16, 64
ML_H, ML_DQK, ML_DV = 8, 64, 128
CHUNK = 64


def _mm(a, b):
    return jnp.dot(a.astype(BF16), b.astype(BF16), preferred_element_type=F32)


def _mm_nt(a, b):
    return lax.dot_general(a.astype(BF16), b.astype(BF16), (((1,), (1,)), ((), ())),
                           preferred_element_type=F32)


def _mm_tn(a, b):
    return lax.dot_general(a.astype(BF16), b.astype(BF16), (((0,), (0,)), ((), ())),
                           preferred_element_type=F32)


def _split3(x):
    x1 = x.astype(BF16).astype(F32)
    r1 = x - x1
    x2 = r1.astype(BF16).astype(F32)
    x3 = (r1 - x2).astype(BF16).astype(F32)
    return x1, x2, x3


def _mm_f32(sel, x):
    return functools.reduce(jnp.add, [_mm(sel, xp) for xp in _split3(x)])


def _mm_nt_f32(sel, x):
    return functools.reduce(jnp.add, [_mm_nt(sel, xp) for xp in _split3(x)])


def _sigmoid(x):
    return 1.0 / (1.0 + jnp.exp(-x))


def _silu(x):
    return x * _sigmoid(x)


def _softplus(x):
    return jnp.maximum(x, 0.0) + jnp.log1p(jnp.exp(-jnp.abs(x)))


def _log_sigmoid(x):
    return -_softplus(-x)


def _iota(shape, dim):
    return lax.broadcasted_iota(jnp.int32, shape, dim)


def _tri(n):
    return (_iota((n, n), 0) >= _iota((n, n), 1)).astype(F32)


def _eye(n, m):
    return (_iota((n, m), 0) == _iota((n, m), 1)).astype(F32)


def _conv_silu(x, halo, w):
    n = x.shape[0]
    xx = jnp.concatenate([halo, x], axis=0)
    y = w[CONV_K - 1:CONV_K] * x
    for j in range(1, CONV_K):
        y = y + w[CONV_K - 1 - j:CONV_K - j] * pltpu.roll(xx, j, 0)[8:8 + n]
    return _silu(y)


def _cparams(sem):
    return pltpu.CompilerParams(dimension_semantics=sem, vmem_limit_bytes=VMEM_LIMIT)


def _bf16_padded(w, mult=LANES):
    pad = (-w.shape[1]) % mult
    wb = w.astype(BF16)
    return jnp.concatenate([wb, jnp.zeros((w.shape[0], pad), BF16)], axis=1) if pad else wb


def _lane_vec(vals, offset):
    return jnp.zeros((1, LANES), F32).at[0, offset:offset + vals.shape[0]].set(vals.astype(F32))


def _const_spec(shape):
    return pl.BlockSpec(shape, lambda *_: (0,) * len(shape), pipeline_mode=pl.Buffered(1))


def _residual_ln(xb, og, w_out, g, b):
    y = ALPHA * xb + _mm(og, w_out)
    mu = jnp.mean(y, axis=-1, keepdims=True)
    yc = y - mu
    var = jnp.mean(yc * yc, axis=-1, keepdims=True)
    return yc * lax.rsqrt(var + LN_EPS) * g + b


def _out_ln_kernel(og_ref, x_ref, w_ref, g_ref, b_ref, o_ref):
    o_ref[...] = _residual_ln(x_ref[...], og_ref[...], w_ref[...], g_ref[...], b_ref[...])


def _out_ln(og, x2, w, g, b):
    n, d = x2.shape
    wd = og.shape[1]
    tm = 512
    return pl.pallas_call(
        _out_ln_kernel,
        grid=(n // tm,),
        in_specs=[pl.BlockSpec((tm, wd), lambda i: (i, 0)),
                  pl.BlockSpec((tm, d), lambda i: (i, 0)),
                  pl.BlockSpec((wd, d), lambda i: (0, 0)),
                  pl.BlockSpec((1, d), lambda i: (0, 0)),
                  pl.BlockSpec((1, d), lambda i: (0, 0))],
        out_specs=pl.BlockSpec((tm, d), lambda i: (i, 0)),
        out_shape=jax.ShapeDtypeStruct((n, d), F32),
        compiler_params=_cparams(("parallel",)),
        name="out_proj_ln",
    )(og, x2, w, g.reshape(1, d).astype(F32), b.reshape(1, d).astype(F32))


GDN_TB = 256
GDN_WAVE = 2


def _gdn_kernel(x_ref, win_ref, cw_ref, alog_ref, dtb_ref, ng_ref, wout_ref, lg_ref, lb_ref,
                o_ref, s_ref, halo_ref, og_ref):
    j = pl.program_id(1)
    c, tb = CHUNK, GDN_TB
    nch = tb // c
    cw = 3 * GDN_H * GDN_DK

    @pl.when(j == 0)
    def _():
        s_ref[...] = jnp.zeros_like(s_ref)
        halo_ref[...] = jnp.zeros_like(halo_ref)

    xb = x_ref[...]
    p = _mm(xb, win_ref[...])

    def conv_tile(col):
        sl = slice(col, col + LANES)
        return _conv_silu(p[:, sl], halo_ref[:, sl], cw_ref[:, sl])

    z = p[:, cw:cw + GDN_H * GDN_DV]

    gat = p[:, cw + GDN_H * GDN_DV:cw + GDN_H * GDN_DV + LANES]
    beta = _sigmoid(gat)
    g = -jnp.exp(alog_ref[...]) * _softplus(gat + dtb_ref[...])
    rt, ct = _iota((tb, tb), 0), _iota((tb, tb), 1)
    tri_bd = ((rt >= ct) & (rt // c == ct // c)).astype(F32)
    gc = _mm_f32(tri_bd, g)
    gct = _mm_nt_f32(_eye(LANES, LANES), gc)

    row = _iota((c, c), 0)
    col = _iota((c, c), 1)
    causal = row >= col
    strict = row > col
    ng = ng_ref[...]
    qoff, koff, voff = 0, GDN_H * GDN_DK, 2 * GDN_H * GDN_DK

    def rows(a, ci):
        return a[ci * c:(ci + 1) * c]

    kn, kb, qn, qd, kd, rhs, gl = {}, {}, {}, {}, {}, {}, {}
    for h in range(GDN_H):
        q = conv_tile(qoff + h * GDN_DK)
        k = conv_tile(koff + h * GDN_DK)
        v = conv_tile(voff + h * GDN_DV)
        q = q * (lax.rsqrt(jnp.sum(q * q, axis=-1, keepdims=True) + EPS) * GDN_DK ** -0.5)
        k = k * lax.rsqrt(jnp.sum(k * k, axis=-1, keepdims=True) + EPS)
        beta_c = beta[:, h:h + 1]
        gc_c = jnp.broadcast_to(gc[:, 8 + h:9 + h], (tb, LANES))
        egc = jnp.exp(gc_c)
        g_last = jnp.concatenate(
            [jnp.broadcast_to(gc_c[(ci + 1) * c - 1:(ci + 1) * c, :], (c, LANES))
             for ci in range(nch)], axis=0)
        kbh = k * beta_c
        kn[h], kb[h], qn[h] = k, kbh, q
        qd[h] = q * egc
        kd[h] = k * jnp.exp(g_last - gc_c)
        rhs[h] = jnp.concatenate([v * beta_c, kbh * egc], axis=1)
        gl[h] = g_last
    halo_ref[...] = p[tb - 8:tb, :cw]

    uw, qk = {}, {}

    def wy_stages(keys):
        neg_l = {}
        for ci, h in keys:
            gmat = _mm_nt(jnp.concatenate([rows(kb[h], ci), rows(qn[h], ci)], axis=0), rows(kn[h], ci))
            gc_c = jnp.broadcast_to(gc[ci * c:(ci + 1) * c, 8 + h:9 + h], (c, c))
            gc_r = gct[8 + h:9 + h, ci * c:(ci + 1) * c]
            decay = jnp.exp(jnp.where(causal, gc_c - gc_r, -jnp.inf))
            neg_l[ci, h] = jnp.where(strict, -(gmat[:c] * decay), 0.0)
            qk[ci, h] = jnp.where(causal, gmat[c:] * decay, 0.0)
        yield
        pw = dict(neg_l)
        m = {key: _mm(neg_l[key], neg_l[key]) for key in keys}
        yield
        nsq = int(math.log2(c)) - 1
        for it in range(nsq):
            if it < nsq - 1:
                r = {key: _mm(jnp.concatenate([pw[key], m[key]], axis=0), m[key]) for key in keys}
                pw = {key: pw[key] + m[key] + r[key][:c] for key in keys}
                m = {key: r[key][c:] for key in keys}
            else:
                pw = {key: pw[key] + m[key] + _mm(pw[key], m[key]) for key in keys}
            yield
        for ci, h in keys:
            rh = rows(rhs[h], ci)
            uw[ci, h] = rh + _mm(pw[ci, h], rh)
        yield

    state = {"s": [s_ref[h] for h in range(GDN_H)]}

    def recurrence(chunks):
        for ci in chunks:
            s = state["s"]
            t = [_mm(jnp.concatenate([uw[ci, h][:, GDN_DV:], rows(qd[h], ci)], axis=0), s[h])
                 for h in range(GDN_H)]
            yield
            v_new = [uw[ci, h][:, :GDN_DV] - t[h][:c] for h in range(GDN_H)]
            o = [t[h][c:] + _mm(qk[ci, h], v_new[h]) for h in range(GDN_H)]
            state["s"] = [s[h] * jnp.exp(gl[h][ci * c:ci * c + 1, :])
                          + _mm_tn(rows(kd[h], ci), v_new[h]) for h in range(GDN_H)]
            yield
            for h in range(GDN_H):
                oh = o[h] * lax.rsqrt(jnp.mean(o[h] * o[h], axis=-1, keepdims=True) + EPS) * ng
                rs, sl = slice(ci * c, (ci + 1) * c), slice(h * GDN_DV, (h + 1) * GDN_DV)
                og_ref[rs, sl] = (oh * _silu(z[rs, sl])).astype(og_ref.dtype)

    groups = [list(range(g, min(g + GDN_WAVE, nch))) for g in range(0, nch, GDN_WAVE)]
    prev = None
    for grp in groups + [None]:
        live = []
        if grp is not None:
            live.append(wy_stages([(ci, h) for ci in grp for h in range(GDN_H)]))
        if prev is not None:
            live.append(recurrence(prev))
        while live:
            for g in list(live):
                if next(g, "done") == "done":
                    live.remove(g)
        prev = grp
    s = state["s"]
    for h in range(GDN_H):
        s_ref[h] = s[h]
    o_ref[...] = _residual_ln(xb, og_ref[...], wout_ref[...], lg_ref[...], lb_ref[...])


def _gdn_layer(x2, bsz, t_len, w_in, conv_w, a_log, dt_bias, norm_g, w_out, ln_g, ln_b):
    n, d = x2.shape
    tb = GDN_TB
    nb = t_len // tb
    cw = 3 * GDN_H * GDN_DK
    w_in = _bf16_padded(w_in)
    row = lambda b, j: (b * nb + j, 0)
    return pl.pallas_call(
        _gdn_kernel,
        grid=(bsz, nb),
        in_specs=[
            pl.BlockSpec((tb, d), row),
            _const_spec(w_in.shape),
            _const_spec((CONV_K, cw)),
            _const_spec((1, LANES)),
            _const_spec((1, LANES)),
            _const_spec((1, LANES)),
            _const_spec(w_out.shape),
            _const_spec((1, d)),
            _const_spec((1, d)),
        ],
        out_specs=pl.BlockSpec((tb, d), row),
        out_shape=jax.ShapeDtypeStruct((n, d), F32),
        scratch_shapes=[pltpu.VMEM((GDN_H, GDN_DK, GDN_DV), F32),
                        pltpu.VMEM((8, cw), F32),
                        pltpu.VMEM((tb, GDN_H * GDN_DV), BF16)],
        compiler_params=_cparams(("arbitrary", "arbitrary")),
        name="gdn_layer",
    )(x2, w_in, conv_w.astype(F32), _lane_vec(a_log, 8), _lane_vec(dt_bias, 8),
      norm_g.reshape(1, GDN_DV).astype(F32), w_out.astype(BF16),
      ln_g.reshape(1, d).astype(F32), ln_b.reshape(1, d).astype(F32))


def _rope_tile(x, cos, sin_signed, first_half):
    rot = jnp.where(first_half, pltpu.roll(x, LANES - SWA_HD // 2, 1), pltpu.roll(x, SWA_HD // 2, 1))
    return x * cos + rot * sin_signed


SWA_TB = 256


def _swa_kernel(sink_ref, x_ref, win_ref, cos_ref, sin_ref, wout_ref, lg_ref, lb_ref,
                o_ref, kp_ref, vp_ref, og_ref):
    j = pl.program_id(1)
    w, tb = SWA_W, SWA_TB
    nw = tb // w
    q_w, kv_w = SWA_H * SWA_HD, SWA_KVH * SWA_HD

    @pl.when(j == 0)
    def _():
        kp_ref[...] = jnp.zeros_like(kp_ref)
        vp_ref[...] = jnp.zeros_like(vp_ref)

    xb = x_ref[...]
    p = _mm(xb, win_ref[...])
    z = p[:, q_w + 2 * kv_w:]
    cos = cos_ref[...]
    sin = sin_ref[...]
    lane = _iota((w, LANES), 1)
    lo = lane < SWA_HD
    lane_t = _iota((tb, LANES), 1)
    first_half = (lane_t % SWA_HD) < SWA_HD // 2
    lo2 = _iota((2 * w, LANES), 1) < SWA_HD

    kr = _rope_tile(p[:, q_w:q_w + kv_w], cos, sin, first_half)
    vc = p[:, q_w + kv_w:q_w + 2 * kv_w]
    kall = jnp.concatenate([kp_ref[...], kr], axis=0)
    vall = jnp.concatenate([vp_ref[...], vc], axis=0)
    kp_ref[...] = kr[tb - w:]
    vp_ref[...] = vc[tb - w:]

    r = _iota((2 * w, 2 * w), 0) % w
    cidx = _iota((2 * w, 2 * w), 1)
    in_prev = (cidx < w) & (cidx > r)
    in_cur = (cidx >= w) & (cidx - w <= r)
    top = _iota((2 * w, 1), 0) < w

    group = SWA_H // SWA_KVH
    tiles = [(wi, t) for wi in range(nw) for t in range(SWA_H // 2)]
    kdup, vdup = {}, {}
    for wi in range(nw):
        kcat = kall[wi * w:(wi + 2) * w]
        vcat = vall[wi * w:(wi + 2) * w]
        kswap = pltpu.roll(kcat, SWA_HD, 1)
        vswap = pltpu.roll(vcat, SWA_HD, 1)
        for kvh in range(SWA_KVH):
            own = lo2 if kvh == 0 else jnp.logical_not(lo2)
            kdup[wi, kvh] = jnp.where(own, kcat, kswap)
            vdup[wi, kvh] = jnp.where(own, vcat, vswap)
    qr = {}
    for t in range(SWA_H // 2):
        qr[t] = _rope_tile(p[:, t * LANES:(t + 1) * LANES], cos, sin, first_half) * SWA_HD ** -0.5
    scores = {}
    for wi, t in tiles:
        qt = qr[t][wi * w:(wi + 1) * w]
        q2 = jnp.concatenate([jnp.where(lo, qt, 0.0), jnp.where(lo, 0.0, qt)], axis=0)
        valid = ((in_prev & (j > 0)) | in_cur) if wi == 0 else (in_prev | in_cur)
        scores[wi, t] = jnp.where(valid, _mm_nt(q2, kdup[wi, t // (group // 2)]), -jnp.inf)
    probs = {}
    for wi, t in tiles:
        s = scores[wi, t]
        sk = jnp.where(top, sink_ref[2 * t], sink_ref[2 * t + 1])
        m = jnp.maximum(jnp.max(s, axis=-1, keepdims=True), sk)
        e = jnp.exp(s - m)
        den = jnp.sum(e, axis=-1, keepdims=True) + jnp.exp(sk - m)
        probs[wi, t] = e / den
    for wi, t in tiles:
        o2 = _mm(probs[wi, t], vdup[wi, t // (group // 2)])
        ot = jnp.where(lo, o2[:w], o2[w:])
        zt = z[wi * w:(wi + 1) * w, t * LANES:(t + 1) * LANES]
        og_ref[wi * w:(wi + 1) * w, t * LANES:(t + 1) * LANES] = (ot * _silu(zt)).astype(og_ref.dtype)
    o_ref[...] = _residual_ln(xb, og_ref[...], wout_ref[...], lg_ref[...], lb_ref[...])


def _swa_layer(x2, bsz, t_len, w_in, sinks, w_out, ln_g, ln_b):
    n, d = x2.shape
    tb = SWA_TB
    nb = t_len // tb
    q_w = SWA_H * SWA_HD
    cos_t, sin_t = _rope_tables(t_len)
    row = lambda b, j: (b * nb + j, 0)
    return pl.pallas_call(
        _swa_kernel,
        grid=(bsz, nb),
        in_specs=[
            pl.BlockSpec(memory_space=pltpu.SMEM),
            pl.BlockSpec((tb, d), row),
            _const_spec(w_in.shape),
            pl.BlockSpec((tb, LANES), lambda b, j: (j, 0)),
            pl.BlockSpec((tb, LANES), lambda b, j: (j, 0)),
            _const_spec(w_out.shape),
            _const_spec((1, d)),
            _const_spec((1, d)),
        ],
        out_specs=pl.BlockSpec((tb, d), row),
        out_shape=jax.ShapeDtypeStruct((n, d), F32),
        scratch_shapes=[pltpu.VMEM((SWA_W, LANES), F32), pltpu.VMEM((SWA_W, LANES), F32),
                        pltpu.VMEM((tb, q_w), BF16)],
        compiler_params=_cparams(("arbitrary", "arbitrary")),
        name="swa_layer",
    )(sinks.astype(F32), x2, w_in.astype(BF16), cos_t, sin_t, w_out.astype(BF16),
      ln_g.reshape(1, d).astype(F32), ln_b.reshape(1, d).astype(F32))


def _rope_tables(t_len):
    half = SWA_HD // 2
    inv = ROPE_THETA ** (-jnp.arange(half, dtype=F32) / half)
    ang = jnp.arange(t_len, dtype=F32)[:, None] * inv[None, :]
    cos, sin = jnp.cos(ang), jnp.sin(ang)
    return (jnp.concatenate([cos, cos, cos, cos], axis=1),
            jnp.concatenate([-sin, sin, -sin, sin], axis=1))


FOX_TB = 256
FOX_VR = FOX_HD + 16


def _fox_prep_kernel(x_ref, win_ref, fb_ref, qg_ref, kg_ref,
                     qa_ref, ka_ref, vt_ref, z_ref, carry_ref):
    j = pl.program_id(1)
    tb = FOX_TB
    wd = FOX_H * FOX_HD

    @pl.when(j == 0)
    def _():
        carry_ref[...] = jnp.zeros_like(carry_ref)

    p = _mm(x_ref[...], win_ref[...])
    z_ref[...] = p[:, 3 * wd:4 * wd]
    logf = _log_sigmoid(p[:, 4 * wd:4 * wd + LANES] + fb_ref[...])
    cum = _mm_f32(_tri(tb), logf) + carry_ref[...]
    carry_ref[...] = cum[tb - 1:tb, :]
    cum2 = cum * LOG2E

    lane = _iota((tb, LANES), 1)
    ones2 = ((_iota((LANES, LANES), 0) // FOX_HD) == (_iota((LANES, LANES), 1) // FOX_HD)).astype(BF16)

    def norm(x, g):
        x2 = x * x
        hi = x2.astype(BF16)
        lo = (x2 - hi.astype(F32)).astype(BF16)
        ss = (jnp.dot(hi, ones2, preferred_element_type=F32)
              + jnp.dot(lo, ones2, preferred_element_type=F32))
        return x * lax.rsqrt(ss * (1.0 / FOX_HD) + EPS) * g

    head_lane = lane < FOX_HD
    bl = lane - FOX_HD
    ones_rows = (_iota((FOX_VR - FOX_HD, tb), 0) == 0).astype(BF16)
    cum_split = _split3(cum2)
    ones_q = ((bl >= 3) & (bl < 6)).astype(F32)
    ones_k = ((bl >= 0) & (bl < 3)).astype(F32)
    for t in range(FOX_H // 2):
        sl = slice(t * LANES, (t + 1) * LANES)
        qn = norm(p[:, t * LANES:(t + 1) * LANES], qg_ref[...]) * (FOX_HD ** -0.5 * LOG2E)
        kn = norm(p[:, wd + t * LANES:wd + (t + 1) * LANES], kg_ref[...])
        vt2 = p[:, 2 * wd + t * LANES:2 * wd + (t + 1) * LANES].T.astype(BF16)
        for half in range(2):
            h = 2 * t + half
            c1, c2, c3 = (jnp.broadcast_to(cs[:, h:h + 1], (tb, LANES)) for cs in cum_split)
            qb = jnp.where(bl == 0, c1, jnp.where(bl == 1, c2, jnp.where(bl == 2, c3, ones_q)))
            kb = jnp.where(bl == 3, -c1, jnp.where(bl == 4, -c2, jnp.where(bl == 5, -c3, ones_k)))
            qh = qn if half == 0 else pltpu.roll(qn, FOX_HD, 1)
            kh = kn if half == 0 else pltpu.roll(kn, FOX_HD, 1)
            qa_ref[:, h * LANES:(h + 1) * LANES] = jnp.where(head_lane, qh, qb).astype(BF16)
            ka_ref[:, h * LANES:(h + 1) * LANES] = jnp.where(head_lane, kh, kb).astype(BF16)
            vt_ref[h * FOX_VR:h * FOX_VR + FOX_HD, :] = vt2[half * FOX_HD:(half + 1) * FOX_HD]
            vt_ref[h * FOX_VR + FOX_HD:(h + 1) * FOX_VR, :] = ones_rows


def _fox_prep(x2, bsz, t_len, w_in, f_bias, q_norm_g, k_norm_g):
    n, d = x2.shape
    tb = FOX_TB
    nb = t_len // tb
    wd = FOX_H * FOX_HD
    w_in = _bf16_padded(w_in)
    row = lambda b, j: (b * nb + j, 0)
    g2 =lambda g: jnp.concatenate([g, g]).reshape(1, LANES).astype(F32)
    return pl.pallas_call(
        _fox_prep_kernel,
        grid=(bsz, nb),
        in_specs=[
            pl.BlockSpec((tb, d), row),
            _const_spec(w_in.shape),
            _const_spec((1, LANES)),
            _const_spec((1, LANES)),
            _const_spec((1, LANES)),
        ],
        out_specs=[
            pl.BlockSpec((tb, 2 * wd), row),
            pl.BlockSpec((tb, 2 * wd), row),
            pl.BlockSpec((FOX_H * FOX_VR, tb), lambda b, j: (b, j)),
            pl.BlockSpec((tb, wd), row),
        ],
        out_shape=[jax.ShapeDtypeStruct((n, 2 * wd), BF16),
                   jax.ShapeDtypeStruct((n, 2 * wd), BF16),
                   jax.ShapeDtypeStruct((bsz * FOX_H * FOX_VR, t_len), BF16),
                   jax.ShapeDtypeStruct((n, wd), F32)],
        scratch_shapes=[pltpu.VMEM((1, LANES), F32)],
        compiler_params=_cparams(("arbitrary", "arbitrary")),
        name="fox_prep",
    )(x2, w_in, _lane_vec(f_bias, 0), g2(q_norm_g), g2(k_norm_g))


FOX_TQ = 512
FOX_HPS = 4
FOX_KC = 16


def _fox_flash_kernel(qa_ref, ka_ref, vt_ref, z_ref, o_ref, s_scr, e_scr, m_scr, acc_scr):
    i = pl.program_id(2)
    tq, nh = FOX_TQ, FOX_HPS
    m_scr[...] = jnp.full(m_scr.shape, -jnp.inf, F32)
    acc_scr[...] = jnp.zeros_like(acc_scr)
    q = [qa_ref[:, hh * LANES:(hh + 1) * LANES] for hh in range(nh)]
    causal_c = _iota((FOX_KC, tq), 0) - _iota((FOX_KC, tq), 1)

    def qk(jb, slot):
        start = pl.multiple_of(jb * tq, tq)
        for hh in range(nh):
            s_scr[slot, hh] = lax.dot_general(
                ka_ref[pl.ds(start, tq), hh * LANES:(hh + 1) * LANES], q[hh],
                (((1,), (1,)), ((), ())), preferred_element_type=F32)

    def softmax_pv(jb, slot, masked):
        start = pl.multiple_of(jb * tq, tq)

        def scores(hh, c):
            s = s_scr[slot, hh, c * FOX_KC:(c + 1) * FOX_KC, :]
            return jnp.where(causal_c <= -c * FOX_KC, s, -jnp.inf) if masked else s

        nkc = tq // FOX_KC
        for hh in range(nh):
            m_prev = m_scr[hh]
            mx = functools.reduce(jnp.maximum, [scores(hh, c) for c in range(nkc)])
            m_new = jnp.maximum(m_prev, jnp.max(mx, axis=0, keepdims=True))
            a = jnp.exp2(m_prev - m_new)
            for c in range(nkc):
                e_scr[hh, c * FOX_KC:(c + 1) * FOX_KC, :] = jnp.exp2(scores(hh, c) - m_new).astype(BF16)
            m_scr[hh] = m_new
            vt = vt_ref[hh * FOX_VR:(hh + 1) * FOX_VR, pl.ds(start, tq)]
            acc_scr[hh] = a * acc_scr[hh] + jnp.dot(vt, e_scr[hh], preferred_element_type=F32)

    qk(0, 0)

    def body(t, carry):
        qk(2 * t + 1, 1)
        softmax_pv(2 * t, 0, False)
        qk(2 * t + 2, 0)
        softmax_pv(2 * t + 1, 1, False)
        return carry

    lax.fori_loop(0, i // 2, body, 0)

    @pl.when(i % 2 == 0)
    def _():
        softmax_pv(i, 0, True)

    @pl.when(i % 2 == 1)
    def _():
        qk(i, 1)
        softmax_pv(i - 1, 0, False)
        softmax_pv(i, 1, True)

    for t in range(nh // 2):
        ot = jnp.concatenate([acc_scr[hh, :FOX_HD] / acc_scr[hh, FOX_HD:FOX_HD + 1]
                              for hh in (2 * t, 2 * t + 1)], axis=0)
        sl = slice(t * LANES, (t + 1) * LANES)
        o_ref[:, sl] = (ot.T * _silu(z_ref[:, sl])).astype(o_ref.dtype)


def _fox_flash(qa, ka, vt, z, bsz, t_len):
    n = qa.shape[0]
    tq = FOX_TQ
    nq = t_len // tq
    wd = FOX_H * FOX_HD
    nh = FOX_HPS
    ng = FOX_H // nh
    return pl.pallas_call(
        _fox_flash_kernel,
        grid=(bsz, ng, nq),
        in_specs=[
            pl.BlockSpec((tq, nh * LANES), lambda b, hg, i: (b * nq + i, hg)),
            pl.BlockSpec((t_len, nh * LANES), lambda b, hg, i: (b, hg)),
            pl.BlockSpec((nh * FOX_VR, t_len), lambda b, hg, i: (b * ng + hg, 0)),
            pl.BlockSpec((tq, nh // 2 * LANES), lambda b, hg, i: (b * nq + i, hg)),
        ],
        out_specs=pl.BlockSpec((tq, nh // 2 * LANES), lambda b, hg, i: (b * nq + i, hg)),
        out_shape=jax.ShapeDtypeStruct((n, wd), BF16),
        scratch_shapes=[pltpu.VMEM((2, nh, tq, tq), F32), pltpu.VMEM((nh, tq, tq), BF16),
                        pltpu.VMEM((nh, 1, tq), F32), pltpu.VMEM((nh, FOX_VR, tq), F32)],
        compiler_params=_cparams(("parallel", "parallel", "arbitrary")),
        name="fox_flash",
    )(qa, ka, vt, z)


ML_TB = 256
ML_WAVE = 2


def _mlstm_kernel(x_ref, win_ref, cw_ref, gb_ref, ng_ref, wout_ref, lg_ref, lb_ref,
                  o_ref, c_ref, m_ref, halo_ref, og_ref):
    j = pl.program_id(1)
    c, tb = CHUNK, ML_TB
    nch = tb // c
    qkw, vw = 2 * ML_H * ML_DQK, ML_H * ML_DV

    @pl.when(j == 0)
    def _():
        c_ref[...] = jnp.zeros_like(c_ref)
        m_ref[...] = jnp.zeros_like(m_ref)
        halo_ref[...] = jnp.zeros_like(halo_ref)

    xb = x_ref[...]
    p = _mm(xb, win_ref[...])
    y = [_conv_silu(p[:, t * LANES:(t + 1) * LANES], halo_ref[:, t * LANES:(t + 1) * LANES],
                    cw_ref[:, t * LANES:(t + 1) * LANES]) for t in range(qkw // LANES)]
    halo_ref[...] = p[tb - 8:tb, :qkw]
    v_all = p[:, qkw:qkw + vw]
    o_gate = p[:, qkw + vw:qkw + 2 * vw]
    z = p[:, qkw + 2 * vw:qkw + 3 * vw]

    gt = p[:, qkw + 3 * vw:qkw + 3 * vw + LANES] + gb_ref[...]
    lane = _iota((tb, LANES), 1)
    rt, ct = _iota((tb, tb), 0), _iota((tb, tb), 1)
    tri_bd = ((rt >= ct) & (rt // c == ct // c)).astype(F32)
    bsum = _mm_f32(tri_bd, _log_sigmoid(gt))
    rows_t = _mm_nt_f32(_eye(LANES, LANES), jnp.where(lane < ML_H, gt, bsum))

    causal = _iota((c, c), 0) >= _iota((c, c), 1)
    lo = lane < ML_DQK
    ones_col = (lane == 0).astype(F32)
    ng = ng_ref[...]
    kbase = ML_H * ML_DQK

    def rows(a, ci):
        return a[ci * c:(ci + 1) * c]

    q, k, v_aug = {}, {}, {}
    e_d, w_inter, e_mt, kw, w_old = {}, {}, {}, {}, {}
    for h in range(ML_H):
        t, half = h // 2, h % 2
        own = lo if half == 0 else jnp.logical_not(lo)
        q[h] = jnp.where(own, y[t], 0.0) * ML_DQK ** -0.5
        k[h] = jnp.where(own, y[kbase // LANES + t], 0.0)
        v_aug[h] = jnp.concatenate([v_all[:, h * ML_DV:(h + 1) * ML_DV], ones_col], axis=1)
        b_c = jnp.broadcast_to(bsum[:, ML_H + h:ML_H + h + 1], (tb, LANES))
        ig_c = jnp.broadcast_to(gt[:, h:h + 1], (tb, LANES))
        m_st = m_ref[h]
        for ci in range(nch):
            bc, igc = rows(b_c, ci), rows(ig_c, ci)
            b_r = rows_t[ML_H + h:ML_H + h + 1, ci * c:(ci + 1) * c]
            ig_r = rows_t[h:h + 1, ci * c:(ci + 1) * c]
            b_last = bc[c - 1:c, :]
            d = jnp.where(causal, bc[:, :c] - b_r + ig_r, -jnp.inf)
            m_intra = jnp.max(d, axis=-1, keepdims=True)
            a_end = b_last - bc + igc
            m_end = jnp.max(a_end, axis=0, keepdims=True)
            inter = bc + m_st
            m_t = jnp.maximum(inter, m_intra)
            m_new = jnp.maximum(b_last + m_st, m_end)
            e_d[ci, h] = jnp.exp(d - m_t[:, :c])
            w_inter[ci, h] = jnp.exp(inter - m_t)
            e_mt[ci, h] = jnp.exp(-m_t)
            w_old[ci, h] = jnp.exp(b_last + m_st - m_new)
            kw[ci, h] = rows(k[h], ci) * jnp.exp(a_end - m_new)
            m_st = m_new
        m_ref[h] = m_st

    groups = [list(range(g, min(g + ML_WAVE, nch))) for g in range(0, nch, ML_WAVE)]
    sc, kv = {}, {}

    def intra(grp):
        for ci in grp:
            for h in range(ML_H):
                sc[ci, h] = _mm_nt(rows(q[h], ci), rows(k[h], ci))
                kv[ci, h] = _mm_tn(kw[ci, h], rows(v_aug[h], ci))

    cst = [c_ref[h] for h in range(ML_H)]
    intra(groups[0])
    order = []
    for gi, grp in enumerate(groups):
        order += [("intra", groups[gi + 1])] if gi + 1 < len(groups) else []
        order += [("final", ci) for ci in grp]
    for kind, arg in order:
        if kind == "intra":
            intra(arg)
            continue
        ci = arg
        qc = [_mm(rows(q[h], ci), cst[h]) for h in range(ML_H)]
        pv = [_mm(sc[ci, h] * e_d[ci, h], rows(v_aug[h], ci)) for h in range(ML_H)]
        for h in range(ML_H):
            wi, wo = w_inter[ci, h], w_old[ci, h]
            num = jnp.concatenate([wi, wi], axis=1) * qc[h] + pv[h]
            cst[h] = jnp.concatenate([wo, wo], axis=1) * cst[h] + kv[ci, h]
            den = num[:, ML_DV:ML_DV + 1]
            hv = num[:, :ML_DV] / jnp.maximum(jnp.abs(den), e_mt[ci, h])
            hv = hv * lax.rsqrt(jnp.mean(hv * hv, axis=-1, keepdims=True) + EPS) * ng
            rs, sl = slice(ci * c, (ci + 1) * c), slice(h * ML_DV, (h + 1) * ML_DV)
            og_ref[rs, sl] = (hv * _sigmoid(o_gate[rs, sl]) * _silu(z[rs, sl])).astype(og_ref.dtype)
    for h in range(ML_H):
        c_ref[h] = cst[h]
    o_ref[...] = _residual_ln(xb, og_ref[...], wout_ref[...], lg_ref[...], lb_ref[...])


def _mlstm_layer(x2, bsz, t_len, w_in, conv_w, gate_bias, norm_g, w_out, ln_g, ln_b):
    n, d = x2.shape
    tb = ML_TB
    nb = t_len // tb
    qkw = 2 * ML_H * ML_DQK
    w_in = _bf16_padded(w_in)
    row = lambda b, j: (b * nb + j, 0)
    return pl.pallas_call(
        _mlstm_kernel,
        grid=(bsz, nb),
        in_specs=[
            pl.BlockSpec((tb, d), row),
            _const_spec(w_in.shape),
            _const_spec((CONV_K, qkw)),
            _const_spec((1, LANES)),
            _const_spec((1, LANES)),
            _const_spec(w_out.shape),
            _const_spec((1, d)),
            _const_spec((1, d)),
        ],
        out_specs=pl.BlockSpec((tb, d), row),
        out_shape=jax.ShapeDtypeStruct((n, d), F32),
        scratch_shapes=[pltpu.VMEM((ML_H, LANES, 2 * ML_DV), F32),
                        pltpu.VMEM((ML_H, 1, LANES), F32),
                        pltpu.VMEM((8, qkw), F32),
                        pltpu.VMEM((tb, ML_H * ML_DV), BF16)],
        compiler_params=_cparams(("arbitrary", "arbitrary")),
        name="mlstm_layer",
    )(x2, w_in, conv_w.astype(F32), _lane_vec(gate_bias, 0), norm_g.reshape(1, ML_DV).astype(F32),
      w_out.astype(BF16), ln_g.reshape(1, d).astype(F32), ln_b.reshape(1, d).astype(F32))


def _fox_layer(x2, bsz, t_len, w_in, f_bias, q_norm_g, k_norm_g, w_out, ln_g, ln_b):
    qa, ka, vt, z = _fox_prep(x2, bsz, t_len, w_in, f_bias, q_norm_g, k_norm_g)
    og = _fox_flash(qa, ka, vt, z, bsz, t_len)
    return _out_ln(og, x2, w_out.astype(BF16), ln_g, ln_b)


def kernel(x, a_w_in, a_conv, a_a_log, a_dt_bias, a_norm_g, a_w_out, b_w_in, b_sinks, b_w_out, c_w_in, c_f_bias, c_q_norm, c_k_norm, c_w_out, d_w_in, d_conv, d_gate_bias, d_norm_g, d_w_out, ln_g, ln_b):
    bsz, t_len, d = x.shape
    x2 = x.reshape(bsz * t_len, d)
    for i in range(DEPTH):
        kind, j = i % 4, i // 4
        if kind == 0:
            x2 = _gdn_layer(x2, bsz, t_len, a_w_in[j], a_conv[j], a_a_log[j], a_dt_bias[j],
                            a_norm_g[j], a_w_out[j], ln_g[i], ln_b[i])
        elif kind == 1:
            x2 = _swa_layer(x2, bsz, t_len, b_w_in[j], b_sinks[j], b_w_out[j], ln_g[i], ln_b[i])
        elif kind == 2:
            x2 = _fox_layer(x2, bsz, t_len, c_w_in[j], c_f_bias[j], c_q_norm[j], c_k_norm[j],
                            c_w_out[j], ln_g[i], ln_b[i])
        else:
            x2 = _mlstm_layer(x2, bsz, t_len, d_w_in[j], d_conv[j], d_gate_bias[j], d_norm_g[j],
                              d_w_out[j], ln_g[i], ln_b[i])
    return x2.reshape(bsz, t_len, d)
```

```python
import functools
import math

import jax
import jax.numpy as jnp
from jax import lax
from jax.experimental import pallas as pl
from jax.experimental.pallas import tpu as pltpu

F32 = jnp.float32
BF16 = jnp.bfloat16
LANES = 128
VMEM_LIMIT = 56 * 1024 * 1024

DEPTH = 4
CONV_K = 4
ROPE_THETA = 10000.0
EPS = 1e-6
LN_EPS = 1e-5
ALPHA = (2 * DEPTH) ** 0.25
LOG2E = math.log2(math.e)

GDN_H, GDN_DK, GDN_DV = 8, 128, 128
SWA_H, SWA_KVH, SWA_HD, SWA_W = 16, 2, 64, 128
FOX_H, FOX_HD = 16, 64
ML_H, ML_DQK, ML_DV = 8, 64, 128
CHUNK = 64


def _mm(a, b):
    return jnp.dot(a.astype(BF16), b.astype(BF16), preferred_element_type=F32)


def _mm_nt(a, b):
    return lax.dot_general(a.astype(BF16), b.astype(BF16), (((1,), (1,)), ((), ())),
                           preferred_element_type=F32)


def _mm_tn(a, b):
    return lax.dot_general(a.astype(BF16), b.astype(BF16), (((0,), (0,)), ((), ())),
                           preferred_element_type=F32)


def _split3(x):
    x1 = x.astype(BF16).astype(F32)
    r1 = x - x1
    x2 = r1.astype(BF16).astype(F32)
    x3 = (r1 - x2).astype(BF16).astype(F32)
    return x1, x2, x3


def _mm_f32(sel, x):
    return functools.reduce(jnp.add, [_mm(sel, xp) for xp in _split3(x)])


def _mm_nt_f32(sel, x):
    return functools.reduce(jnp.add, [_mm_nt(sel, xp) for xp in _split3(x)])


def _sigmoid(x):
    return 1.0 / (1.0 + jnp.exp(-x))


def _silu(x):
    return x * _sigmoid(x)


def _softplus(x):
    return jnp.maximum(x, 0.0) + jnp.log1p(jnp.exp(-jnp.abs(x)))


def _log_sigmoid(x):
    return -_softplus(-x)


def _iota(shape, dim):
    return lax.broadcasted_iota(jnp.int32, shape, dim)


def _tri(n):
    return (_iota((n, n), 0) >= _iota((n, n), 1)).astype(F32)


def _eye(n, m):
    return (_iota((n, m), 0) == _iota((n, m), 1)).astype(F32)


def _conv_silu(x, halo, w):
    n = x.shape[0]
    xx = jnp.concatenate([halo, x], axis=0)
    y = w[CONV_K - 1:CONV_K] * x
    for j in range(1, CONV_K):
        y = y + w[CONV_K - 1 - j:CONV_K - j] * pltpu.roll(xx, j, 0)[8:8 + n]
    return _silu(y)


def _cparams(sem):
    return pltpu.CompilerParams(dimension_semantics=sem, vmem_limit_bytes=VMEM_LIMIT)


def _bf16_padded(w, mult=LANES):
    pad = (-w.shape[1]) % mult
    wb = w.astype(BF16)
    return jnp.concatenate([wb, jnp.zeros((w.shape[0], pad), BF16)], axis=1) if pad else wb


def _lane_vec(vals, offset):
    return jnp.zeros((1, LANES), F32).at[0, offset:offset + vals.shape[0]].set(vals.astype(F32))


def _const_spec(shape):
    return pl.BlockSpec(shape, lambda *_: (0,) * len(shape), pipeline_mode=pl.Buffered(1))


def _residual_ln(xb, og, w_out, g, b):
    y = ALPHA * xb + _mm(og, w_out)
    mu = jnp.mean(y, axis=-1, keepdims=True)
    yc = y - mu
    var = jnp.mean(yc * yc, axis=-1, keepdims=True)
    return yc * lax.rsqrt(var + LN_EPS) * g + b


def _out_ln_kernel(og_ref, x_ref, w_ref, g_ref, b_ref, o_ref):
    o_ref[...] = _residual_ln(x_ref[...], og_ref[...], w_ref[...], g_ref[...], b_ref[...])


def _out_ln(og, x2, w, g, b):
    n, d = x2.shape
    wd = og.shape[1]
    tm = 512
    return pl.pallas_call(
        _out_ln_kernel,
        grid=(n // tm,),
        in_specs=[pl.BlockSpec((tm, wd), lambda i: (i, 0)),
                  pl.BlockSpec((tm, d), lambda i: (i, 0)),
                  pl.BlockSpec((wd, d), lambda i: (0, 0)),
                  pl.BlockSpec((1, d), lambda i: (0, 0)),
                  pl.BlockSpec((1, d), lambda i: (0, 0))],
        out_specs=pl.BlockSpec((tm, d), lambda i: (i, 0)),
        out_shape=jax.ShapeDtypeStruct((n, d), F32),
        compiler_params=_cparams(("parallel",)),
        name="out_proj_ln",
    )(og, x2, w, g.reshape(1, d).astype(F32), b.reshape(1, d).astype(F32))


GDN_TB = 256
GDN_WAVE = 2


def _gdn_kernel(x_ref, win_ref, cw_ref, alog_ref, dtb_ref, ng_ref, wout_ref, lg_ref, lb_ref,
                o_ref, s_ref, halo_ref, og_ref):
    j = pl.program_id(1)
    c, tb = CHUNK, GDN_TB
    nch = tb // c
    cw = 3 * GDN_H * GDN_DK

    @pl.when(j == 0)
    def _():
        s_ref[...] = jnp.zeros_like(s_ref)
        halo_ref[...] = jnp.zeros_like(halo_ref)

    xb = x_ref[...]
    p = _mm(xb, win_ref[...])

    def conv_tile(col):
        sl = slice(col, col + LANES)
        return _conv_silu(p[:, sl], halo_ref[:, sl], cw_ref[:, sl])

    z = p[:, cw:cw + GDN_H * GDN_DV]

    gat = p[:, cw + GDN_H * GDN_DV:cw + GDN_H * GDN_DV + LANES]
    beta = _sigmoid(gat)
    g = -jnp.exp(alog_ref[...]) * _softplus(gat + dtb_ref[...])
    rt, ct = _iota((tb, tb), 0), _iota((tb, tb), 1)
    tri_bd = ((rt >= ct) & (rt // c == ct // c)).astype(F32)
    gc = _mm_f32(tri_bd, g)
    gct = _mm_nt_f32(_eye(LANES, LANES), gc)

    row = _iota((c, c), 0)
    col = _iota((c, c), 1)
    causal = row >= col
    strict = row > col
    ng = ng_ref[...]
    qoff, koff, voff = 0, GDN_H * GDN_DK, 2 * GDN_H * GDN_DK

    def rows(a, ci):
        return a[ci * c:(ci + 1) * c]

    kn, kb, qn, qd, kd, rhs, gl = {}, {}, {}, {}, {}, {}, {}
    for h in range(GDN_H):
        q = conv_tile(qoff + h * GDN_DK)
        k = conv_tile(koff + h * GDN_DK)
        v = conv_tile(voff + h * GDN_DV)
        q = q * (lax.rsqrt(jnp.sum(q * q, axis=-1, keepdims=True) + EPS) * GDN_DK ** -0.5)
        k = k * lax.rsqrt(jnp.sum(k * k, axis=-1, keepdims=True) + EPS)
        beta_c = beta[:, h:h + 1]
        gc_c = jnp.broadcast_to(gc[:, 8 + h:9 + h], (tb, LANES))
        egc = jnp.exp(gc_c)
        g_last = jnp.concatenate(
            [jnp.broadcast_to(gc_c[(ci + 1) * c - 1:(ci + 1) * c, :], (c, LANES))
             for ci in range(nch)], axis=0)
        kbh = k * beta_c
        kn[h], kb[h], qn[h] = k, kbh, q
        qd[h] = q * egc
        kd[h] = k * jnp.exp(g_last - gc_c)
        rhs[h] = jnp.concatenate([v * beta_c, kbh * egc], axis=1)
        gl[h] = g_last
    halo_ref[...] = p[tb - 8:tb, :cw]

    uw, qk = {}, {}

    def wy_stages(keys):
        neg_l = {}
        for ci, h in keys:
            gmat = _mm_nt(jnp.concatenate([rows(kb[h], ci), rows(qn[h], ci)], axis=0), rows(kn[h], ci))
            gc_c = jnp.broadcast_to(gc[ci * c:(ci + 1) * c, 8 + h:9 + h], (c, c))
            gc_r = gct[8 + h:9 + h, ci * c:(ci + 1) * c]
            decay = jnp.exp(jnp.where(causal, gc_c - gc_r, -jnp.inf))
            neg_l[ci, h] = jnp.where(strict, -(gmat[:c] * decay), 0.0)
            qk[ci, h] = jnp.where(causal, gmat[c:] * decay, 0.0)
        yield
        pw = dict(neg_l)
        m = {key: _mm(neg_l[key], neg_l[key]) for key in keys}
        yield
        nsq = int(math.log2(c)) - 1
        for it in range(nsq):
            if it < nsq - 1:
                r = {key: _mm(jnp.concatenate([pw[key], m[key]], axis=0), m[key]) for key in keys}
                pw = {key: pw[key] + m[key] + r[key][:c] for key in keys}
                m = {key: r[key][c:] for key in keys}
            else:
                pw = {key: pw[key] + m[key] + _mm(pw[key], m[key]) for key in keys}
            yield
        for ci, h in keys:
            rh = rows(rhs[h], ci)
            uw[ci, h] = rh + _mm(pw[ci, h], rh)
        yield

    state = {"s": [s_ref[h] for h in range(GDN_H)]}

    def recurrence(chunks):
        for ci in chunks:
            s = state["s"]
            t = [_mm(jnp.concatenate([uw[ci, h][:, GDN_DV:], rows(qd[h], ci)], axis=0), s[h])
                 for h in range(GDN_H)]
            yield
            v_new = [uw[ci, h][:, :GDN_DV] - t[h][:c] for h in range(GDN_H)]
            o = [t[h][c:] + _mm(qk[ci, h], v_new[h]) for h in range(GDN_H)]
            state["s"] = [s[h] * jnp.exp(gl[h][ci * c:ci * c + 1, :])
                          + _mm_tn(rows(kd[h], ci), v_new[h]) for h in range(GDN_H)]
            yield
            for h in range(GDN_H):
                oh = o[h] * lax.rsqrt(jnp.mean(o[h] * o[h], axis=-1, keepdims=True) + EPS) * ng
                rs, sl = slice(ci * c, (ci + 1) * c), slice(h * GDN_DV, (h + 1) * GDN_DV)
                og_ref[rs, sl] = (oh * _silu(z[rs, sl])).astype(og_ref.dtype)

    groups = [list(range(g, min(g + GDN_WAVE, nch))) for g in range(0, nch, GDN_WAVE)]
    prev = None
    for grp in groups + [None]:
        live = []
        if grp is not None:
            live.append(wy_stages([(ci, h) for ci in grp for h in range(GDN_H)]))
        if prev is not None:
            live.append(recurrence(prev))
        while live:
            for g in list(live):
                if next(g, "done") == "done":
                    live.remove(g)
        prev = grp
    s = state["s"]
    for h in range(GDN_H):
        s_ref[h] = s[h]
    o_ref[...] = _residual_ln(xb, og_ref[...], wout_ref[...], lg_ref[...], lb_ref[...])


def _gdn_layer(x2, bsz, t_len, w_in, conv_w, a_log, dt_bias, norm_g, w_out, ln_g, ln_b):
    n, d = x2.shape
    tb = GDN_TB
    nb = t_len // tb
    cw = 3 * GDN_H * GDN_DK
    w_in = _bf16_padded(w_in)
    row = lambda b, j: (b * nb + j, 0)
    return pl.pallas_call(
        _gdn_kernel,
        grid=(bsz, nb),
        in_specs=[
            pl.BlockSpec((tb, d), row),
            _const_spec(w_in.shape),
            _const_spec((CONV_K, cw)),
            _const_spec((1, LANES)),
            _const_spec((1, LANES)),
            _const_spec((1, LANES)),
            _const_spec(w_out.shape),
            _const_spec((1, d)),
            _const_spec((1, d)),
        ],
        out_specs=pl.BlockSpec((tb, d), row),
        out_shape=jax.ShapeDtypeStruct((n, d), F32),
        scratch_shapes=[pltpu.VMEM((GDN_H, GDN_DK, GDN_DV), F32),
                        pltpu.VMEM((8, cw), F32),
                        pltpu.VMEM((tb, GDN_H * GDN_DV), BF16)],
        compiler_params=_cparams(("arbitrary", "arbitrary")),
        name="gdn_layer",
    )(x2, w_in, conv_w.astype(F32), _lane_vec(a_log, 8), _lane_vec(dt_bias, 8),
      norm_g.reshape(1, GDN_DV).astype(F32), w_out.astype(BF16),
      ln_g.reshape(1, d).astype(F32), ln_b.reshape(1, d).astype(F32))


def _rope_tile(x, cos, sin_signed, first_half):
    rot = jnp.where(first_half, pltpu.roll(x, LANES - SWA_HD // 2, 1), pltpu.roll(x, SWA_HD // 2, 1))
    return x * cos + rot * sin_signed


SWA_TB = 256


def _swa_kernel(sink_ref, x_ref, win_ref, cos_ref, sin_ref, wout_ref, lg_ref, lb_ref,
                o_ref, kp_ref, vp_ref, og_ref):
    j = pl.program_id(1)
    w, tb = SWA_W, SWA_TB
    nw = tb // w
    q_w, kv_w = SWA_H * SWA_HD, SWA_KVH * SWA_HD

    @pl.when(j == 0)
    def _():
        kp_ref[...] = jnp.zeros_like(kp_ref)
        vp_ref[...] = jnp.zeros_like(vp_ref)

    xb = x_ref[...]
    p = _mm(xb, win_ref[...])
    z = p[:, q_w + 2 * kv_w:]
    cos = cos_ref[...]
    sin = sin_ref[...]
    lane = _iota((w, LANES), 1)
    lo = lane < SWA_HD
    lane_t = _iota((tb, LANES), 1)
    first_half = (lane_t % SWA_HD) < SWA_HD // 2
    lo2 = _iota((2 * w, LANES), 1) < SWA_HD

    kr = _rope_tile(p[:, q_w:q_w + kv_w], cos, sin, first_half)
    vc = p[:, q_w + kv_w:q_w + 2 * kv_w]
    kall = jnp.concatenate([kp_ref[...], kr], axis=0)
    vall = jnp.concatenate([vp_ref[...], vc], axis=0)
    kp_ref[...] = kr[tb - w:]
    vp_ref[...] = vc[tb - w:]

    r = _iota((2 * w, 2 * w), 0) % w
    cidx = _iota((2 * w, 2 * w), 1)
    in_prev = (cidx < w) & (cidx > r)
    in_cur = (cidx >= w) & (cidx - w <= r)
    top = _iota((2 * w, 1), 0) < w

    group = SWA_H // SWA_KVH
    tiles = [(wi, t) for wi in range(nw) for t in range(SWA_H // 2)]
    kdup, vdup = {}, {}
    for wi in range(nw):
        kcat = kall[wi * w:(wi + 2) * w]
        vcat = vall[wi * w:(wi + 2) * w]
        kswap = pltpu.roll(kcat, SWA_HD, 1)
        vswap = pltpu.roll(vcat, SWA_HD, 1)
        for kvh in range(SWA_KVH):
            own = lo2 if kvh == 0 else jnp.logical_not(lo2)
            kdup[wi, kvh] = jnp.where(own, kcat, kswap)
            vdup[wi, kvh] = jnp.where(own, vcat, vswap)
    qr = {}
    for t in range(SWA_H // 2):
        qr[t] = _rope_tile(p[:, t * LANES:(t + 1) * LANES], cos, sin, first_half) * SWA_HD ** -0.5
    scores = {}
    for wi, t in tiles:
        qt = qr[t][wi * w:(wi + 1) * w]
        q2 = jnp.concatenate([jnp.where(lo, qt, 0.0), jnp.where(lo, 0.0, qt)], axis=0)
        valid = ((in_prev & (j > 0)) | in_cur) if wi == 0 else (in_prev | in_cur)
        scores[wi, t] = jnp.where(valid, _mm_nt(q2, kdup[wi, t // (group // 2)]), -jnp.inf)
    probs = {}
    for wi, t in tiles:
        s = scores[wi, t]
        sk = jnp.where(top, sink_ref[2 * t], sink_ref[2 * t + 1])
        m = jnp.maximum(jnp.max(s, axis=-1, keepdims=True), sk)
        e = jnp.exp(s - m)
        den = jnp.sum(e, axis=-1, keepdims=True) + jnp.exp(sk - m)
        probs[wi, t] = e / den
    for wi, t in tiles:
        o2 = _mm(probs[wi, t], vdup[wi, t // (group // 2)])
        ot = jnp.where(lo, o2[:w], o2[w:])
        zt = z[wi * w:(wi + 1) * w, t * LANES:(t + 1) * LANES]
        og_ref[wi * w:(wi + 1) * w, t * LANES:(t + 1) * LANES] = (ot * _silu(zt)).astype(og_ref.dtype)
    o_ref[...] = _residual_ln(xb, og_ref[...], wout_ref[...], lg_ref[...], lb_ref[...])


def _swa_layer(x2, bsz, t_len, w_in, sinks, w_out, ln_g, ln_b):
    n, d = x2.shape
    tb = SWA_TB
    nb = t_len // tb
    q_w = SWA_H * SWA_HD
    cos_t, sin_t = _rope_tables(t_len)
    row = lambda b, j: (b * nb + j, 0)
    return pl.pallas_call(
        _swa_kernel,
        grid=(bsz, nb),
        in_specs=[
            pl.BlockSpec(memory_space=pltpu.SMEM),
            pl.BlockSpec((tb, d), row),
            _const_spec(w_in.shape),
            pl.BlockSpec((tb, LANES), lambda b, j: (j, 0)),
            pl.BlockSpec((tb, LANES), lambda b, j: (j, 0)),
            _const_spec(w_out.shape),
            _const_spec((1, d)),
            _const_spec((1, d)),
        ],
        out_specs=pl.BlockSpec((tb, d), row),
        out_shape=jax.ShapeDtypeStruct((n, d), F32),
        scratch_shapes=[pltpu.VMEM((SWA_W, LANES), F32), pltpu.VMEM((SWA_W, LANES), F32),
                        pltpu.VMEM((tb, q_w), BF16)],
        compiler_params=_cparams(("arbitrary", "arbitrary")),
        name="swa_layer",
    )(sinks.astype(F32), x2, w_in.astype(BF16), cos_t, sin_t, w_out.astype(BF16),
      ln_g.reshape(1, d).astype(F32), ln_b.reshape(1, d).astype(F32))


def _rope_tables(t_len):
    half = SWA_HD // 2
    inv = ROPE_THETA ** (-jnp.arange(half, dtype=F32) / half)
    ang = jnp.arange(t_len, dtype=F32)[:, None] * inv[None, :]
    cos, sin = jnp.cos(ang), jnp.sin(ang)
    return (jnp.concatenate([cos, cos, cos, cos], axis=1),
            jnp.concatenate([-sin, sin, -sin, sin], axis=1))


FOX_TB = 256
FOX_VR = FOX_HD + 16


def _fox_bias_selectors():
    import numpy as np
    sq = np.zeros((FOX_H // 2, LANES, 2 * LANES), np.float32)
    sk = np.zeros((FOX_H // 2, LANES, 2 * LANES), np.float32)
    for t in range(FOX_H // 2):
        for half in range(2):
            h, base = 2 * t + half, half * LANES + FOX_HD
            for i in range(3):
                sq[t, i * FOX_H + h, base + i] = 1.0
                sq[t, 3 * FOX_H, base + 3 + i] = 1.0
                sk[t, 3 * FOX_H, base + i] = 1.0
                sk[t, i * FOX_H + h, base + 3 + i] = -1.0
    return jnp.asarray(sq, BF16), jnp.asarray(sk, BF16)


def _fox_prep_kernel(x_ref, win_ref, fb_ref, qg_ref, kg_ref, sq_ref, sk_ref,
                     qa_ref, ka_ref, vt_ref, z_ref, carry_ref):
    j = pl.program_id(1)
    tb = FOX_TB
    wd = FOX_H * FOX_HD

    @pl.when(j == 0)
    def _():
        carry_ref[...] = jnp.zeros_like(carry_ref)

    p = _mm(x_ref[...], win_ref[...])
    z_ref[...] = p[:, 3 * wd:4 * wd]
    logf = _log_sigmoid(p[:, 4 * wd:4 * wd + LANES] + fb_ref[...])
    cum = _mm_f32(_tri(tb), logf) + carry_ref[...]
    carry_ref[...] = cum[tb - 1:tb, :]
    cum2 = cum * LOG2E

    lane = _iota((tb, LANES), 1)
    ones2 = ((_iota((LANES, LANES), 0) // FOX_HD) == (_iota((LANES, LANES), 1) // FOX_HD)).astype(BF16)

    def norm(x, g):
        x2 = x * x
        hi = x2.astype(BF16)
        lo = (x2 - hi.astype(F32)).astype(BF16)
        ss = (jnp.dot(hi, ones2, preferred_element_type=F32)
              + jnp.dot(lo, ones2, preferred_element_type=F32))
        return x * lax.rsqrt(ss * (1.0 / FOX_HD) + EPS) * g

    head_lane = lane < FOX_HD
    bl = lane - FOX_HD
    ones_rows = (_iota((FOX_VR - FOX_HD, tb), 0) == 0).astype(BF16)
    c1, c2, c3 = _split3(cum2)
    packed = jnp.where(lane < FOX_H, c1,
                       jnp.where(lane < 2 * FOX_H, pltpu.roll(c2, FOX_H, 1),
                                 jnp.where(lane < 3 * FOX_H, pltpu.roll(c3, 2 * FOX_H, 1),
                                           (lane == 3 * FOX_H).astype(F32))))
    for t in range(FOX_H // 2):
        qb2 = _mm(packed, sq_ref[t])
        kb2 = _mm(packed, sk_ref[t])
        sl = slice(t * LANES, (t + 1) * LANES)
        qn = norm(p[:, t * LANES:(t + 1) * LANES], qg_ref[...]) * (FOX_HD ** -0.5 * LOG2E)
        kn = norm(p[:, wd + t * LANES:wd + (t + 1) * LANES], kg_ref[...])
        vt2 = p[:, 2 * wd + t * LANES:2 * wd + (t + 1) * LANES].T.astype(BF16)
        for half in range(2):
            h = 2 * t + half
            qb = qb2[:, half * LANES:(half + 1) * LANES]
            kb = kb2[:, half * LANES:(half + 1) * LANES]
            qh = qn if half == 0 else pltpu.roll(qn, FOX_HD, 1)
            kh = kn if half == 0 else pltpu.roll(kn, FOX_HD, 1)
            qa_ref[:, h * LANES:(h + 1) * LANES] = jnp.where(head_lane, qh, qb).astype(BF16)
            ka_ref[:, h * LANES:(h + 1) * LANES] = jnp.where(head_lane, kh, kb).astype(BF16)
            vt_ref[h * FOX_VR:h * FOX_VR + FOX_HD, :] = vt2[half * FOX_HD:(half + 1) * FOX_HD]
            vt_ref[h * FOX_VR + FOX_HD:(h + 1) * FOX_VR, :] = ones_rows


def _fox_prep(x2, bsz, t_len, w_in, f_bias, q_norm_g, k_norm_g):
    n, d = x2.shape
    tb = FOX_TB
    nb = t_len // tb
    wd = FOX_H * FOX_HD
    w_in = _bf16_padded(w_in)
    row = lambda b, j: (b * nb + j, 0)
    g2 =lambda g: jnp.concatenate([g, g]).reshape(1, LANES).astype(F32)
    return pl.pallas_call(
        _fox_prep_kernel,
        grid=(bsz, nb),
        in_specs=[
            pl.BlockSpec((tb, d), row),
            _const_spec(w_in.shape),
            _const_spec((1, LANES)),
            _const_spec((1, LANES)),
            _const_spec((1, LANES)),
            _const_spec((FOX_H // 2, LANES, 2 * LANES)),
            _const_spec((FOX_H // 2, LANES, 2 * LANES)),
        ],
        out_specs=[
            pl.BlockSpec((tb, 2 * wd), row),
            pl.BlockSpec((tb, 2 * wd), row),
            pl.BlockSpec((FOX_H * FOX_VR, tb), lambda b, j: (b, j)),
            pl.BlockSpec((tb, wd), row),
        ],
        out_shape=[jax.ShapeDtypeStruct((n, 2 * wd), BF16),
                   jax.ShapeDtypeStruct((n, 2 * wd), BF16),
                   jax.ShapeDtypeStruct((bsz * FOX_H * FOX_VR, t_len), BF16),
                   jax.ShapeDtypeStruct((n, wd), F32)],
        scratch_shapes=[pltpu.VMEM((1, LANES), F32)],
        compiler_params=_cparams(("arbitrary", "arbitrary")),
        name="fox_prep",
    )(x2, w_in, _lane_vec(f_bias, 0), g2(q_norm_g), g2(k_norm_g), *_fox_bias_selectors())


FOX_TQ = 512
FOX_HPS = 4
FOX_KC = 16


def _fox_flash_kernel(qa_ref, ka_ref, vt_ref, z_ref, o_ref, s_scr, e_scr, m_scr, acc_scr):
    i = pl.program_id(2)
    tq, nh = FOX_TQ, FOX_HPS
    m_scr[...] = jnp.full(m_scr.shape, -jnp.inf, F32)
    acc_scr[...] = jnp.zeros_like(acc_scr)
    q = [qa_ref[:, hh * LANES:(hh + 1) * LANES] for hh in range(nh)]
    causal_c = _iota((FOX_KC, tq), 0) - _iota((FOX_KC, tq), 1)

    def qk(jb, slot):
        start = pl.multiple_of(jb * tq, tq)
        for hh in range(nh):
            s_scr[slot, hh] = lax.dot_general(
                ka_ref[pl.ds(start, tq), hh * LANES:(hh + 1) * LANES], q[hh],
                (((1,), (1,)), ((), ())), preferred_element_type=F32)

    def softmax_pv(jb, slot, masked):
        start = pl.multiple_of(jb * tq, tq)

        def scores(hh, c):
            s = s_scr[slot, hh, c * FOX_KC:(c + 1) * FOX_KC, :]
            return jnp.where(causal_c <= -c * FOX_KC, s, -jnp.inf) if masked else s

        nkc = tq // FOX_KC
        for hh in range(nh):
            m_prev = m_scr[hh]
            mx = functools.reduce(jnp.maximum, [scores(hh, c) for c in range(nkc)])
            m_new = jnp.maximum(m_prev, jnp.max(mx, axis=0, keepdims=True))
            a = jnp.exp2(m_prev - m_new)
            for c in range(nkc):
                e_scr[hh, c * FOX_KC:(c + 1) * FOX_KC, :] = jnp.exp2(scores(hh, c) - m_new).astype(BF16)
            m_scr[hh] = m_new
            vt = vt_ref[hh * FOX_VR:(hh + 1) * FOX_VR, pl.ds(start, tq)]
            acc_scr[hh] = a * acc_scr[hh] + jnp.dot(vt, e_scr[hh], preferred_element_type=F32)

    qk(0, 0)

    def body(t, carry):
        qk(2 * t + 1, 1)
        softmax_pv(2 * t, 0, False)
        qk(2 * t + 2, 0)
        softmax_pv(2 * t + 1, 1, False)
        return carry

    lax.fori_loop(0, i // 2, body, 0)

    @pl.when(i % 2 == 0)
    def _():
        softmax_pv(i, 0, True)

    @pl.when(i % 2 == 1)
    def _():
        qk(i, 1)
        softmax_pv(i - 1, 0, False)
        softmax_pv(i, 1, True)

    for t in range(nh // 2):
        ot = jnp.concatenate([acc_scr[hh, :FOX_HD] / acc_scr[hh, FOX_HD:FOX_HD + 1]
                              for hh in (2 * t, 2 * t + 1)], axis=0)
        sl = slice(t * LANES, (t + 1) * LANES)
        o_ref[:, sl] = (ot.T * _silu(z_ref[:, sl])).astype(o_ref.dtype)


def _fox_flash(qa, ka, vt, z, bsz, t_len):
    n = qa.shape[0]
    tq = FOX_TQ
    nq = t_len // tq
    wd = FOX_H * FOX_HD
    nh = FOX_HPS
    ng = FOX_H // nh
    return pl.pallas_call(
        _fox_flash_kernel,
        grid=(bsz, ng, nq),
        in_specs=[
            pl.BlockSpec((tq, nh * LANES), lambda b, hg, i: (b * nq + i, hg)),
            pl.BlockSpec((t_len, nh * LANES), lambda b, hg, i: (b, hg)),
            pl.BlockSpec((nh * FOX_VR, t_len), lambda b, hg, i: (b * ng + hg, 0)),
            pl.BlockSpec((tq, nh // 2 * LANES), lambda b, hg, i: (b * nq + i, hg)),
        ],
        out_specs=pl.BlockSpec((tq, nh // 2 * LANES), lambda b, hg, i: (b * nq + i, hg)),
        out_shape=jax.ShapeDtypeStruct((n, wd), BF16),
        scratch_shapes=[pltpu.VMEM((2, nh, tq, tq), F32), pltpu.VMEM((nh, tq, tq), BF16),
                        pltpu.VMEM((nh, 1, tq), F32), pltpu.VMEM((nh, FOX_VR, tq), F32)],
        compiler_params=_cparams(("parallel", "parallel", "arbitrary")),
        name="fox_flash",
    )(qa, ka, vt, z)


ML_TB = 256
ML_WAVE = 2


def _mlstm_kernel(x_ref, win_ref, cw_ref, gb_ref, ng_ref, wout_ref, lg_ref, lb_ref,
                  o_ref, c_ref, m_ref, halo_ref, og_ref):
    j = pl.program_id(1)
    c, tb = CHUNK, ML_TB
    nch = tb // c
    qkw, vw = 2 * ML_H * ML_DQK, ML_H * ML_DV

    @pl.when(j == 0)
    def _():
        c_ref[...] = jnp.zeros_like(c_ref)
        m_ref[...] = jnp.zeros_like(m_ref)
        halo_ref[...] = jnp.zeros_like(halo_ref)

    xb = x_ref[...]
    p = _mm(xb, win_ref[...])
    y = [_conv_silu(p[:, t * LANES:(t + 1) * LANES], halo_ref[:, t * LANES:(t + 1) * LANES],
                    cw_ref[:, t * LANES:(t + 1) * LANES]) for t in range(qkw // LANES)]
    halo_ref[...] = p[tb - 8:tb, :qkw]
    v_all = p[:, qkw:qkw + vw]
    o_gate = p[:, qkw + vw:qkw + 2 * vw]
    z = p[:, qkw + 2 * vw:qkw + 3 * vw]

    gt = p[:, qkw + 3 * vw:qkw + 3 * vw + LANES] + gb_ref[...]
    lane = _iota((tb, LANES), 1)
    rt, ct = _iota((tb, tb), 0), _iota((tb, tb), 1)
    tri_bd = ((rt >= ct) & (rt // c == ct // c)).astype(F32)
    bsum = _mm_f32(tri_bd, _log_sigmoid(gt))
    rows_t = _mm_nt_f32(_eye(LANES, LANES), jnp.where(lane < ML_H, gt, bsum))

    causal = _iota((c, c), 0) >= _iota((c, c), 1)
    lo = lane < ML_DQK
    ones_col = (lane == 0).astype(F32)
    ng = ng_ref[...]
    kbase = ML_H * ML_DQK

    def rows(a, ci):
        return a[ci * c:(ci + 1) * c]

    q, k, v_aug = {}, {}, {}
    e_d, w_inter, e_mt, kw, w_old = {}, {}, {}, {}, {}
    for h in range(ML_H):
        t, half = h // 2, h % 2
        own = lo if half == 0 else jnp.logical_not(lo)
        q[h] = jnp.where(own, y[t], 0.0) * ML_DQK ** -0.5
        k[h] = jnp.where(own, y[kbase // LANES + t], 0.0)
        v_aug[h] = jnp.concatenate([v_all[:, h * ML_DV:(h + 1) * ML_DV], ones_col], axis=1)
        b_c = jnp.broadcast_to(bsum[:, ML_H + h:ML_H + h + 1], (tb, LANES))
        ig_c = jnp.broadcast_to(gt[:, h:h + 1], (tb, LANES))
        m_st = m_ref[h]
        for ci in range(nch):
            bc, igc = rows(b_c, ci), rows(ig_c, ci)
            b_r = rows_t[ML_H + h:ML_H + h + 1, ci * c:(ci + 1) * c]
            ig_r = rows_t[h:h + 1, ci * c:(ci + 1) * c]
            b_last = bc[c - 1:c, :]
            d = jnp.where(causal, bc[:, :c] - b_r + ig_r, -jnp.inf)
            m_intra = jnp.max(d, axis=-1, keepdims=True)
            a_end = b_last - bc + igc
            m_end = jnp.max(a_end, axis=0, keepdims=True)
            inter = bc + m_st
            m_t = jnp.maximum(inter, m_intra)
            m_new = jnp.maximum(b_last + m_st, m_end)
            e_d[ci, h] = jnp.exp(d - m_t[:, :c])
            w_inter[ci, h] = jnp.exp(inter - m_t)
            e_mt[ci, h] = jnp.exp(-m_t)
            w_old[ci, h] = jnp.exp(b_last + m_st - m_new)
            kw[ci, h] = rows(k[h], ci) * jnp.exp(a_end - m_new)
            m_st = m_new
        m_ref[h] = m_st

    groups = [list(range(g, min(g + ML_WAVE, nch))) for g in range(0, nch, ML_WAVE)]
    sc, kv = {}, {}

    def intra(grp):
        for ci in grp:
            for h in range(ML_H):
                sc[ci, h] = _mm_nt(rows(q[h], ci), rows(k[h], ci))
                kv[ci, h] = _mm_tn(kw[ci, h], rows(v_aug[h], ci))

    cst = [c_ref[h] for h in range(ML_H)]
    intra(groups[0])
    order = []
    for gi, grp in enumerate(groups):
        order += [("intra", groups[gi + 1])] if gi + 1 < len(groups) else []
        order += [("final", ci) for ci in grp]
    for kind, arg in order:
        if kind == "intra":
            intra(arg)
            continue
        ci = arg
        qc = [_mm(rows(q[h], ci), cst[h]) for h in range(ML_H)]
        pv = [_mm(sc[ci, h] * e_d[ci, h], rows(v_aug[h], ci)) for h in range(ML_H)]
        for h in range(ML_H):
            wi, wo = w_inter[ci, h], w_old[ci, h]
            num = jnp.concatenate([wi, wi], axis=1) * qc[h] + pv[h]
            cst[h] = jnp.concatenate([wo, wo], axis=1) * cst[h] + kv[ci, h]
            den = num[:, ML_DV:ML_DV + 1]
            hv = num[:, :ML_DV] / jnp.maximum(jnp.abs(den), e_mt[ci, h])
            hv = hv * lax.rsqrt(jnp.mean(hv * hv, axis=-1, keepdims=True) + EPS) * ng
            rs, sl = slice(ci * c, (ci + 1) * c), slice(h * ML_DV, (h + 1) * ML_DV)
            og_ref[rs, sl] = (hv * _sigmoid(o_gate[rs, sl]) * _silu(z[rs, sl])).astype(og_ref.dtype)
    for h in range(ML_H):
        c_ref[h] = cst[h]
    o_ref[...] = _residual_ln(xb, og_ref[...], wout_ref[...], lg_ref[...], lb_ref[...])


def _mlstm_layer(x2, bsz, t_len, w_in, conv_w, gate_bias, norm_g, w_out, ln_g, ln_b):
    n, d = x2.shape
    tb = ML_TB
    nb = t_len // tb
    qkw = 2 * ML_H * ML_DQK
    w_in = _bf16_padded(w_in)
    row = lambda b, j: (b * nb + j, 0)
    return pl.pallas_call(
        _mlstm_kernel,
        grid=(bsz, nb),
        in_specs=[
            pl.BlockSpec((tb, d), row),
            _const_spec(w_in.shape),
            _const_spec((CONV_K, qkw)),
            _const_spec((1, LANES)),
            _const_spec((1, LANES)),
            _const_spec(w_out.shape),
            _const_spec((1, d)),
            _const_spec((1, d)),
        ],
        out_specs=pl.BlockSpec((tb, d), row),
        out_shape=jax.ShapeDtypeStruct((n, d), F32),
        scratch_shapes=[pltpu.VMEM((ML_H, LANES, 2 * ML_DV), F32),
                        pltpu.VMEM((ML_H, 1, LANES), F32),
                        pltpu.VMEM((8, qkw), F32),
                        pltpu.VMEM((tb, ML_H * ML_DV), BF16)],
        compiler_params=_cparams(("arbitrary", "arbitrary")),
        name="mlstm_layer",
    )(x2, w_in, conv_w.astype(F32), _lane_vec(gate_bias, 0), norm_g.reshape(1, ML_DV).astype(F32),
      w_out.astype(BF16), ln_g.reshape(1, d).astype(F32), ln_b.reshape(1, d).astype(F32))


def _fox_layer(x2, bsz, t_len, w_in, f_bias, q_norm_g, k_norm_g, w_out, ln_g, ln_b):
    qa, ka, vt, z = _fox_prep(x2, bsz, t_len, w_in, f_bias, q_norm_g, k_norm_g)
    og = _fox_flash(qa, ka, vt, z, bsz, t_len)
    return _out_ln(og, x2, w_out.astype(BF16), ln_g, ln_b)


def kernel(x, a_w_in, a_conv, a_a_log, a_dt_bias, a_norm_g, a_w_out, b_w_in, b_sinks, b_w_out, c_w_in, c_f_bias, c_q_norm, c_k_norm, c_w_out, d_w_in, d_conv, d_gate_bias, d_norm_g, d_w_out, ln_g, ln_b):
    bsz, t_len, d = x.shape
    x2 = x.reshape(bsz * t_len, d)
    for i in range(DEPTH):
        kind, j = i % 4, i // 4
        if kind == 0:
            x2 = _gdn_layer(x2, bsz, t_len, a_w_in[j], a_conv[j], a_a_log[j], a_dt_bias[j],
                            a_norm_g[j], a_w_out[j], ln_g[i], ln_b[i])
        elif kind == 1:
            x2 = _swa_layer(x2, bsz, t_len, b_w_in[j], b_sinks[j], b_w_out[j], ln_g[i], ln_b[i])
        elif kind == 2:
            x2 = _fox_layer(x2, bsz, t_len, c_w_in[j], c_f_bias[j], c_q_norm[j], c_k_norm[j],
                            c_w_out[j], ln_g[i], ln_b[i])
        else:
            x2 = _mlstm_layer(x2, bsz, t_len, d_w_in[j], d_conv[j], d_gate_bias[j], d_norm_g[j],
                              d_w_out[j], ln_g[i], ln_b[i])
    return x2.reshape(bsz, t_len, d)
```
